```python
import jax, jax.numpy as jnp
from jax import lax
import numpy as np

D_MODEL = 1024
BATCH = 4
SEQ = 4096
DEPTH = 1

D_MIX = 2 * D_MODEL
D_FNET = D_MIX // 4
N_FNET_GROUPS = 4
FNET_GROUP = D_FNET // N_FNET_GROUPS
D_SSM = D_MIX - D_FNET
SSM_HEAD_DIM = 64
N_SSM_HEADS = D_SSM // SSM_HEAD_DIM
N_BC_GROUPS = 4
HEADS_PER_GROUP = N_SSM_HEADS // N_BC_GROUPS
D_STATE = 128
CONV_WIDTH = 5
SSD_CHUNK = 128
D_CONV = D_SSM + 2 * N_BC_GROUPS * D_STATE
D_IN_PROJ = D_FNET + D_SSM + D_CONV + 2 * N_SSM_HEADS

PEER_HEADS = 8
PEER_KEYS = 128
PEER_EXPERTS = PEER_KEYS * PEER_KEYS
PEER_QUERY_DIM = 256
PEER_HALF = PEER_QUERY_DIM // 2
PEER_TOPK = 16
PEER_TOKEN_BLOCK = 128

EPS = 1e-6
DT_MIN = 0.001
DT_MAX = 0.1

kernel_name = "fnet_ssd_peer_adaln_hybrid"


def rms_norm(x, gain):
    xf = x.astype(jnp.float32)
    y = xf * lax.rsqrt(jnp.mean(xf * xf, axis=-1, keepdims=True) + EPS)
    return (y * gain.astype(jnp.float32)).astype(x.dtype)


def modulate(h, shift, scale):
    return h * (1 + scale[:, None, :]) + shift[:, None, :]


def fnet_mix(h):
    b, s, _ = h.shape
    hg = h.astype(jnp.float32).reshape(b, s, N_FNET_GROUPS, FNET_GROUP)
    f = jnp.fft.fft2(hg, axes=(1, 3), norm="ortho").real
    return f.reshape(b, s, D_FNET).astype(h.dtype)


def centred_depthwise_conv(u, w, bias):
    pad = CONV_WIDTH // 2
    out = lax.conv_general_dilated(
        u, w[:, None, :], window_strides=(1,), padding=[(pad, pad)],
        dimension_numbers=("NWC", "WIO", "NWC"), feature_group_count=u.shape[-1])
    return out + bias


def segsum(a):
    t = a.shape[-1]
    rep = jnp.broadcast_to(a[..., :, None], a.shape + (t,))
    strict = jnp.tril(jnp.ones((t, t), dtype=bool), -1)
    cs = jnp.cumsum(jnp.where(strict, rep, 0.0), axis=-2)
    lower = jnp.tril(jnp.ones((t, t), dtype=bool), 0)
    return jnp.where(lower, cs, -jnp.inf)


def ssd_scan(xdt, adt, bm, cm):
    b, s, g, r, p = xdt.shape
    l = SSD_CHUNK
    c = s // l
    X = xdt.reshape(b, c, l, g, r, p)
    A = jnp.transpose(adt.reshape(b, c, l, g, r), (0, 3, 4, 1, 2))
    Bc = bm.reshape(b, c, l, g, -1)
    Cc = cm.reshape(b, c, l, g, -1)
    A_cs = jnp.cumsum(A, axis=-1)
    Lmat = jnp.exp(segsum(A))
    CB = jnp.einsum("bclgn,bcsgn->bgcls", Cc, Bc)
    y_diag = jnp.einsum("bgcls,bgrcls,bcsgrp->bclgrp", CB, Lmat, X)
    decay_states = jnp.exp(A_cs[..., -1:] - A_cs)
    states = jnp.einsum("bclgn,bgrcl,bclgrp->bcgrpn", Bc, decay_states, X)
    states = jnp.concatenate([jnp.zeros_like(states[:, :1]), states], axis=1)
    chunk_tot = jnp.pad(A_cs[..., -1], ((0, 0), (0, 0), (0, 0), (1, 0)))
    decay_chunk = jnp.exp(segsum(chunk_tot))
    new_states = jnp.einsum("bgrzc,bcgrpn->bzgrpn", decay_chunk, states)
    states = new_states[:, :-1]
    y_off = jnp.einsum("bclgn,bcgrpn,bgrcl->bclgrp", Cc, states, jnp.exp(A_cs))
    return (y_diag + y_off).reshape(b, s, g, r, p)


def ssd_direction(xs, bm, cm, dt_raw, a_log, dt_bias):
    b, s = xs.shape[:2]
    dt = jax.nn.softplus(dt_raw + dt_bias)
    a = -jnp.exp(a_log)
    xdt = (xs * dt[..., None]).reshape(b, s, N_BC_GROUPS, HEADS_PER_GROUP, SSM_HEAD_DIM)
    adt = (dt * a).reshape(b, s, N_BC_GROUPS, HEADS_PER_GROUP)
    return ssd_scan(xdt, adt, bm, cm).reshape(b, s, N_SSM_HEADS, SSM_HEAD_DIM)


def bidirectional_ssd(z, xbc, dt_raw, conv_w, conv_b, a_log_fwd, a_log_bwd,
                      dt_bias_fwd, dt_bias_bwd, d_skip, ssm_norm_g):
    b, s, _ = xbc.shape
    xbc = jax.nn.silu(centred_depthwise_conv(xbc, conv_w, conv_b)).astype(jnp.float32)
    xs = xbc[..., :D_SSM].reshape(b, s, N_SSM_HEADS, SSM_HEAD_DIM)
    bm = xbc[..., D_SSM:D_SSM + N_BC_GROUPS * D_STATE].reshape(b, s, N_BC_GROUPS, D_STATE)
    cm = xbc[..., D_SSM + N_BC_GROUPS * D_STATE:].reshape(b, s, N_BC_GROUPS, D_STATE)
    dt_raw = dt_raw.astype(jnp.float32)
    y_f = ssd_direction(xs, bm, cm, dt_raw[..., :N_SSM_HEADS],
                        a_log_fwd.astype(jnp.float32), dt_bias_fwd.astype(jnp.float32))
    y_b = jnp.flip(ssd_direction(jnp.flip(xs, 1), jnp.flip(bm, 1), jnp.flip(cm, 1),
                                 jnp.flip(dt_raw[..., N_SSM_HEADS:], 1),
                                 a_log_bwd.astype(jnp.float32), dt_bias_bwd.astype(jnp.float32)), 1)
    y = y_f + y_b + xs * d_skip.astype(jnp.float32)[:, None]
    y = y.reshape(b, s, D_SSM) * jax.nn.silu(z.astype(jnp.float32))
    yg = y.reshape(b, s, N_BC_GROUPS, D_SSM // N_BC_GROUPS)
    yg = yg * lax.rsqrt(jnp.mean(yg * yg, axis=-1, keepdims=True) + EPS)
    y = yg.reshape(b, s, D_SSM) * ssm_norm_g.astype(jnp.float32)
    return y.astype(z.dtype)


def peer_ffn(h, w_query, sub_keys, expert_down, expert_up):
    b, s, d = h.shape
    tokens = h.reshape(-1, PEER_TOKEN_BLOCK, d)
    keys32 = sub_keys.astype(jnp.float32)

    def block(hb):
        q = (hb @ w_query).astype(jnp.float32).reshape(PEER_TOKEN_BLOCK, PEER_HEADS, 2, PEER_HALF)
        scores = jnp.einsum("thjc,hjkc->thjk", q, keys32)
        s1, i1 = lax.top_k(scores[:, :, 0], PEER_TOPK)
        s2, i2 = lax.top_k(scores[:, :, 1], PEER_TOPK)
        cand_s = (s1[..., :, None] + s2[..., None, :]).reshape(PEER_TOKEN_BLOCK, PEER_HEADS, -1)
        cand_i = (i1[..., :, None] * PEER_KEYS + i2[..., None, :]).reshape(PEER_TOKEN_BLOCK, PEER_HEADS, -1)
        top_s, pos = lax.top_k(cand_s, PEER_TOPK)
        idx = jnp.take_along_axis(cand_i, pos, axis=-1)
        gates = jax.nn.softmax(top_s, axis=-1)
        u = expert_down[idx]
        v = expert_up[idx]
        act = jnp.einsum("td,thkd->thk", hb, u).astype(jnp.float32)
        w = (gates * jax.nn.gelu(act, approximate=False)).astype(hb.dtype)
        return jnp.einsum("thk,thkd->td", w, v)

    out = lax.map(block, tokens)
    return out.reshape(b, s, d)


def setup_inputs(seed: int = 0) -> dict:
    key = jax.random.key(seed)
    ks = jax.random.split(key, 24)
    f32 = jnp.float32
    nrm = lambda k, shape, scale: jax.random.normal(k, shape, f32) * scale
    dt = jnp.exp(jax.random.uniform(ks[9], (DEPTH, N_SSM_HEADS), f32,
                                    np.log(DT_MIN), np.log(DT_MAX)))
    dt2 = jnp.exp(jax.random.uniform(ks[10], (DEPTH, N_SSM_HEADS), f32,
                                     np.log(DT_MIN), np.log(DT_MAX)))
    return {
        "x": nrm(ks[0], (BATCH, SEQ, D_MODEL), 1.0),
        "c": nrm(ks[1], (BATCH, D_MODEL), 1.0),
        "w_ada": nrm(ks[2], (DEPTH, D_MODEL, 6 * D_MODEL), 0.5 * D_MODEL ** -0.5),
        "b_ada": nrm(ks[3], (DEPTH, 6 * D_MODEL), 0.02),
        "norm_mix_g": 1.0 + nrm(ks[4], (DEPTH, D_MODEL), 0.05),
        "w_in": nrm(ks[5], (DEPTH, D_MODEL, D_IN_PROJ), D_MODEL ** -0.5),
        "conv_w": nrm(ks[6], (DEPTH, CONV_WIDTH, D_CONV), CONV_WIDTH ** -0.5),
        "conv_b": nrm(ks[7], (DEPTH, D_CONV), 0.02),
        "a_log_fwd": jnp.log(jax.random.uniform(ks[8], (DEPTH, N_SSM_HEADS), f32, 1.0, 16.0)),
        "a_log_bwd": jnp.log(jax.random.uniform(ks[11], (DEPTH, N_SSM_HEADS), f32, 1.0, 16.0)),
        "dt_bias_fwd": dt + jnp.log(-jnp.expm1(-dt)),
        "dt_bias_bwd": dt2 + jnp.log(-jnp.expm1(-dt2)),
        "d_skip": 1.0 + nrm(ks[12], (DEPTH, N_SSM_HEADS), 0.1),
        "ssm_norm_g": 1.0 + nrm(ks[13], (DEPTH, D_SSM), 0.05),
        "w_out": nrm(ks[14], (DEPTH, D_MIX, D_MODEL), D_MIX ** -0.5),
        "norm_ffn_g": 1.0 + nrm(ks[15], (DEPTH, D_MODEL), 0.05),
        "w_query": nrm(ks[16], (DEPTH, D_MODEL, PEER_HEADS * PEER_QUERY_DIM), D_MODEL ** -0.5),
        "sub_keys": nrm(ks[17], (DEPTH, PEER_HEADS, 2, PEER_KEYS, PEER_HALF), PEER_HALF ** -0.5),
        "expert_down": nrm(ks[18], (DEPTH, PEER_EXPERTS, D_MODEL), D_MODEL ** -0.5),
        "expert_up": nrm(ks[19], (DEPTH, PEER_EXPERTS, D_MODEL), 1.0),
        "final_norm_g": 1.0 + nrm(ks[20], (D_MODEL,), 0.05),
    }


def reference(x, c, w_ada, b_ada, norm_mix_g, w_in, conv_w, conv_b, a_log_fwd, a_log_bwd,
              dt_bias_fwd, dt_bias_bwd, d_skip, ssm_norm_g, w_out, norm_ffn_g, w_query,
              sub_keys, expert_down, expert_up, final_norm_g):
    c_act = jax.nn.silu(c)
    for layer in range(DEPTH):
        mod = jnp.einsum("bd,de->be", c_act, w_ada[layer]) + b_ada[layer]
        shift_m, scale_m, gate_m, shift_f, scale_f, gate_f = jnp.split(mod, 6, axis=-1)

        h = modulate(rms_norm(x, norm_mix_g[layer]), shift_m, scale_m)
        proj = jnp.einsum("bsd,de->bse", h, w_in[layer])
        f_in = proj[..., :D_FNET]
        z = proj[..., D_FNET:D_FNET + D_SSM]
        xbc = proj[..., D_FNET + D_SSM:D_FNET + D_SSM + D_CONV]
        dt_raw = proj[..., D_FNET + D_SSM + D_CONV:]
        y_fnet = fnet_mix(f_in)
        y_ssm = bidirectional_ssd(z, xbc, dt_raw, conv_w[layer], conv_b[layer],
                                  a_log_fwd[layer], a_log_bwd[layer], dt_bias_fwd[layer],
                                  dt_bias_bwd[layer], d_skip[layer], ssm_norm_g[layer])
        mixed = jnp.concatenate([y_fnet, y_ssm], axis=-1)
        x = x + gate_m[:, None, :] * jnp.einsum("bse,ed->bsd", mixed, w_out[layer])

        h2 = modulate(rms_norm(x, norm_ffn_g[layer]), shift_f, scale_f)
        x = x + gate_f[:, None, :] * peer_ffn(h2, w_query[layer], sub_keys[layer],
                                              expert_down[layer], expert_up[layer])
    return rms_norm(x, final_norm_g)
```

```python
import functools
import math

import jax
import jax.numpy as jnp
from jax import lax
from jax.experimental import pallas as pl
from jax.experimental.pallas import tpu as pltpu

F32 = jnp.float32
BF16 = jnp.bfloat16
HIGHEST = lax.Precision.HIGHEST

D_MODEL = 1024
BATCH = 4
SEQ = 4096
TOKENS = BATCH * SEQ
D_MIX = 2 * D_MODEL
D_FNET = D_MIX // 4
N_FNET_GROUPS = 4
FNET_GROUP = D_FNET // N_FNET_GROUPS
D_SSM = D_MIX - D_FNET
SSM_HEAD_DIM = 64
N_SSM_HEADS = D_SSM // SSM_HEAD_DIM
N_BC_GROUPS = 4
HEADS_PER_GROUP = N_SSM_HEADS // N_BC_GROUPS
D_STATE = 128
CONV_WIDTH = 5
SSD_CHUNK = 128
N_CHUNKS = SEQ // SSD_CHUNK
D_BC = N_BC_GROUPS * D_STATE
D_CONV = D_SSM + 2 * D_BC
GROUP_WIDTH = D_SSM // N_BC_GROUPS
PEER_HEADS = 8
PEER_KEYS = 128
PEER_EXPERTS = PEER_KEYS * PEER_KEYS
PEER_HALF = 128
PEER_TOPK = 16
EPS = 1e-6

LANES = 128
SUBLANES = 8
DT_PAD = LANES
VMEM_LIMIT = 56 * 1024 * 1024

ADA_TN = 512
PROJ_TM = 512
DFT_TM = 512
CONV_TC = 256
CONV_TR = 512
CONV_HALO = SUBLANES
OUT_TM = 512
PLAN_TT = 256
DENSE_TT = 512
DENSE_TE = 1024


def _cparams(sem):
    return pltpu.CompilerParams(dimension_semantics=sem, vmem_limit_bytes=VMEM_LIMIT)


def _sigmoid(x):
    return 1.0 / (1.0 + jnp.exp(-x))


def _softplus(x):
    return jnp.maximum(x, 0.0) + jnp.log(1.0 + jnp.exp(-jnp.abs(x)))


def _adaln_kernel(c_ref, w_ref, b_ref, o_ref):
    c = c_ref[...]
    ca = c * _sigmoid(c)
    o_ref[...] = jnp.dot(ca, w_ref[...], preferred_element_type=F32, precision=HIGHEST) + b_ref[...]


def _adaln(c_pad, w_ada, b_ada):
    n = w_ada.shape[1]
    return pl.pallas_call(
        _adaln_kernel,
        grid=(n // ADA_TN,),
        in_specs=[
            pl.BlockSpec((SUBLANES, D_MODEL), lambda j: (0, 0)),
            pl.BlockSpec((D_MODEL, ADA_TN), lambda j: (0, j)),
            pl.BlockSpec((1, ADA_TN), lambda j: (0, j)),
        ],
        out_specs=pl.BlockSpec((SUBLANES, ADA_TN), lambda j: (0, j)),
        out_shape=jax.ShapeDtypeStruct((SUBLANES, n), F32),
        compiler_params=_cparams(("arbitrary",)),
        name="adaln",
    )(c_pad, w_ada, b_ada)


def _modulated_norm(x, gain, shift, scale):
    ms = jnp.mean(x * x, axis=-1, keepdims=True)
    xn = x * lax.rsqrt(ms + EPS) * gain
    return xn * (1.0 + scale) + shift


def _inproj_kernel(x_ref, g_ref, shift_ref, scale_ref, wf_ref, wz_ref, wx_ref, wdt_ref, wdtT_ref, csc_ref,
                   uc_ref, us_ref, z_ref, xbc_ref, dt_ref, dtT_ref):
    h = _modulated_norm(x_ref[...], g_ref[...], shift_ref[0], scale_ref[0])
    hb = h.astype(BF16)
    fb = jnp.dot(hb, wf_ref[...], preferred_element_type=F32).astype(BF16)
    for g in range(N_FNET_GROUPS):
        sl = slice(g * FNET_GROUP, (g + 1) * FNET_GROUP)
        u = jnp.dot(fb[:, sl], csc_ref[...], preferred_element_type=F32)
        uc_ref[:, sl] = u[:, :FNET_GROUP].astype(BF16)
        us_ref[:, sl] = u[:, FNET_GROUP:].astype(BF16)
    z_ref[...] = jnp.dot(hb, wz_ref[...], preferred_element_type=F32).astype(BF16)
    xbc_ref[...] = jnp.dot(hb, wx_ref[...], preferred_element_type=F32).astype(BF16)
    dt_ref[...] = jnp.dot(hb, wdt_ref[...], preferred_element_type=F32)
    dtT_ref[...] = lax.dot_general(wdtT_ref[...], hb, (((1,), (1,)), ((), ())), preferred_element_type=F32)


def _inproj(x2d, gain, shift, scale, wf, wz, wx, wdt, wdtT, csc):
    tiles_per_batch = SEQ // PROJ_TM
    row = lambda i: (i, 0)
    const = lambda i: (0, 0)
    per_batch = lambda i: (i // tiles_per_batch, 0, 0)
    return pl.pallas_call(
        _inproj_kernel,
        grid=(TOKENS // PROJ_TM,),
        in_specs=[
            pl.BlockSpec((PROJ_TM, D_MODEL), row),
            pl.BlockSpec((1, D_MODEL), const),
            pl.BlockSpec((1, 1, D_MODEL), per_batch),
            pl.BlockSpec((1, 1, D_MODEL), per_batch),
            pl.BlockSpec(wf.shape, const),
            pl.BlockSpec(wz.shape, const),
            pl.BlockSpec(wx.shape, const),
            pl.BlockSpec(wdt.shape, const),
            pl.BlockSpec(wdtT.shape, const),
            pl.BlockSpec(csc.shape, const),
        ],
        out_specs=[
            pl.BlockSpec((PROJ_TM, D_FNET), row),
            pl.BlockSpec((PROJ_TM, D_FNET), row),
            pl.BlockSpec((PROJ_TM, D_SSM), row),
            pl.BlockSpec((PROJ_TM, D_CONV), row),
            pl.BlockSpec((PROJ_TM, DT_PAD), row),
            pl.BlockSpec((DT_PAD, PROJ_TM), lambda i: (0, i)),
        ],
        out_shape=[
            jax.ShapeDtypeStruct((TOKENS, D_FNET), BF16),
            jax.ShapeDtypeStruct((TOKENS, D_FNET), BF16),
            jax.ShapeDtypeStruct((TOKENS, D_SSM), BF16),
            jax.ShapeDtypeStruct((TOKENS, D_CONV), BF16),
            jax.ShapeDtypeStruct((TOKENS, DT_PAD), F32),
            jax.ShapeDtypeStruct((DT_PAD, TOKENS), F32),
        ],
        compiler_params=_cparams(("arbitrary",)),
        name="inproj",
    )(x2d, gain, shift, scale, wf, wz, wx, wdt, wdtT, csc)


def _dft_kernel(cs_ref, ss_ref, uc_ref, us_ref, o_ref):
    y = jnp.dot(cs_ref[...], uc_ref[...], preferred_element_type=F32)
    y = y - jnp.dot(ss_ref[...], us_ref[...], preferred_element_type=F32)
    o_ref[...] = y.astype(BF16)


def _dft(cs, ss, uc, us):
    mt = SEQ // DFT_TM
    return pl.pallas_call(
        _dft_kernel,
        grid=(BATCH, mt),
        in_specs=[
            pl.BlockSpec((DFT_TM, SEQ), lambda b, m: (m, 0)),
            pl.BlockSpec((DFT_TM, SEQ), lambda b, m: (m, 0)),
            pl.BlockSpec((SEQ, D_FNET), lambda b, m: (b, 0)),
            pl.BlockSpec((SEQ, D_FNET), lambda b, m: (b, 0)),
        ],
        out_specs=pl.BlockSpec((DFT_TM, D_FNET), lambda b, m: (b * mt + m, 0)),
        out_shape=jax.ShapeDtypeStruct((TOKENS, D_FNET), BF16),
        compiler_params=_cparams(("arbitrary", "arbitrary")),
        name="fnet_dft",
    )(cs, ss, uc, us)


def _conv_kernel(x_ref, w_ref, b_ref, o_ref, pad_ref):
    zeros = jnp.zeros((CONV_HALO, CONV_TC), F32)
    pad_ref[0:CONV_HALO, :] = zeros
    pad_ref[SEQ + CONV_HALO:SEQ + 2 * CONV_HALO, :] = zeros
    pad_ref[CONV_HALO:SEQ + CONV_HALO, :] = x_ref[0].astype(F32)
    w = w_ref[...]
    bias = b_ref[...]
    first = CONV_HALO - CONV_WIDTH // 2
    for i in range(SEQ // CONV_TR):
        acc = jnp.broadcast_to(bias, (CONV_TR, CONV_TC))
        for k in range(CONV_WIDTH):
            lo = first + k + i * CONV_TR
            acc = acc + pad_ref[lo:lo + CONV_TR, :] * w[k:k + 1, :]
        o_ref[0, i * CONV_TR:(i + 1) * CONV_TR, :] = (acc * _sigmoid(acc)).astype(BF16)


def _conv(xbc3, conv_w, conv_b):
    return pl.pallas_call(
        _conv_kernel,
        grid=(BATCH, D_CONV // CONV_TC),
        in_specs=[
            pl.BlockSpec((1, SEQ, CONV_TC), lambda b, j: (b, 0, j)),
            pl.BlockSpec((CONV_WIDTH, CONV_TC), lambda b, j: (0, j)),
            pl.BlockSpec((1, CONV_TC), lambda b, j: (0, j)),
        ],
        out_specs=pl.BlockSpec((1, SEQ, CONV_TC), lambda b, j: (b, 0, j)),
        out_shape=jax.ShapeDtypeStruct((BATCH, SEQ, D_CONV), BF16),
        scratch_shapes=[pltpu.VMEM((SEQ + 2 * CONV_HALO, CONV_TC), F32)],
        compiler_params=_cparams(("arbitrary", "arbitrary")),
        name="conv_silu",
    )(xbc3, conv_w, conv_b)


def _ssd_direction(xc, dtv, cs, csT, a, aT, state_ref, yacc_ref, head_base, forward):
    li = lax.broadcasted_iota(jnp.int32, (SSD_CHUNK, SSD_CHUNK), 0)
    si = lax.broadcasted_iota(jnp.int32, (SSD_CHUNK, SSD_CHUNK), 1)
    if forward:
        u, uT = cs, csT
        mask = li >= si
    else:
        u, uT = cs - a, csT - aT
        mask = si >= li
    tot_row = cs[SSD_CHUNK - 1:SSD_CHUNK, :]
    for g in range(N_BC_GROUPS):
        bc = xc[:, D_SSM + g * D_STATE:D_SSM + (g + 1) * D_STATE]
        cc = xc[:, D_SSM + D_BC + g * D_STATE:D_SSM + D_BC + (g + 1) * D_STATE]
        gmat = lax.dot_general(cc, bc, (((1,), (1,)), ((), ())), preferred_element_type=F32)
        bcT = bc.astype(F32).T.astype(BF16)
        s_prev = state_ref[g]
        cstate = jnp.dot(cc, s_prev.astype(BF16), preferred_element_type=F32)
        for r in range(HEADS_PER_GROUP):
            h = g * HEADS_PER_GROUP + r
            hh = head_base + h
            ps = slice(r * SSM_HEAD_DIM, (r + 1) * SSM_HEAD_DIM)
            ucol = u[:, hh:hh + 1]
            urow = uT[hh:hh + 1, :]
            tot = tot_row[:, hh:hh + 1]
            diff = (ucol - urow) if forward else (urow - ucol)
            lmat = jnp.where(mask, jnp.exp(jnp.minimum(diff, 0.0)), 0.0)
            m = (gmat * lmat).astype(BF16)
            xh = xc[:, h * SSM_HEAD_DIM:(h + 1) * SSM_HEAD_DIM].astype(F32)
            xdt = xh * dtv[:, hh:hh + 1]
            ydiag = jnp.dot(m, xdt.astype(BF16), preferred_element_type=F32)
            if forward:
                off_scale = jnp.exp(ucol)
                w_state = jnp.exp(tot - ucol)
            else:
                off_scale = jnp.exp(tot - ucol)
                w_state = jnp.exp(ucol)
            yacc_ref[:, h * SSM_HEAD_DIM:(h + 1) * SSM_HEAD_DIM] = ydiag + off_scale * cstate[:, ps]
            upd = jnp.dot(bcT, (xdt * w_state).astype(BF16), preferred_element_type=F32)
            state_ref[g, :, ps] = s_prev[:, ps] * jnp.exp(tot) + upd


def _ssd_kernel(xc_ref, dt_ref, dtT_ref, z_ref, alog_row_ref, bias_row_ref, alog_col_ref, bias_col_ref,
                dskip_ref, ng_ref, y_ref, state_ref, yacc_ref, yb_ref):
    phase = pl.program_id(1)
    step = pl.program_id(2)

    @pl.when(step == 0)
    def _():
        state_ref[...] = jnp.zeros(state_ref.shape, F32)

    xc = xc_ref[...]
    dtv = _softplus(dt_ref[...] + bias_row_ref[...])
    a = dtv * (-jnp.exp(alog_row_ref[...]))
    dtvT = _softplus(dtT_ref[...] + bias_col_ref[...])
    aT = dtvT * (-jnp.exp(alog_col_ref[...]))
    li = lax.broadcasted_iota(jnp.int32, (SSD_CHUNK, SSD_CHUNK), 0)
    si = lax.broadcasted_iota(jnp.int32, (SSD_CHUNK, SSD_CHUNK), 1)
    lower = (li >= si).astype(F32)
    upper = (li <= si).astype(F32)
    cs = jnp.dot(lower, a, preferred_element_type=F32, precision=HIGHEST)
    csT = jnp.dot(aT, upper, preferred_element_type=F32, precision=HIGHEST)

    @pl.when(phase == 0)
    def _():
        chunk = N_CHUNKS - 1 - step
        _ssd_direction(xc, dtv, cs, csT, a, aT, state_ref, yacc_ref, N_SSM_HEADS, False)
        yb_ref[chunk] = yacc_ref[...].astype(BF16)

    @pl.when(phase == 1)
    def _():
        _ssd_direction(xc, dtv, cs, csT, a, aT, state_ref, yacc_ref, 0, True)
        xs = xc[:, :D_SSM].astype(F32)
        y = yacc_ref[...] + yb_ref[step].astype(F32) + xs * dskip_ref[...]
        z = z_ref[...].astype(F32)
        y = y * (z * _sigmoid(z))
        for g in range(N_BC_GROUPS):
            gs = slice(g * GROUP_WIDTH, (g + 1) * GROUP_WIDTH)
            yg = y[:, gs]
            ms = jnp.mean(yg * yg, axis=-1, keepdims=True)
            y_ref[:, gs] = (yg * lax.rsqrt(ms + EPS) * ng_ref[:, gs]).astype(BF16)


def _ssd(xconv, dt, dtT, z, alog_row, bias_row, alog_col, bias_col, dskip_row, norm_g):
    def chunk_of(p, s):
        return p * s + (1 - p) * (N_CHUNKS - 1 - s)

    rows = lambda b, p, s: (b * N_CHUNKS + chunk_of(p, s), 0)
    cols = lambda b, p, s: (0, b * N_CHUNKS + chunk_of(p, s))
    fwd_only = lambda b, p, s: (b * N_CHUNKS + p * s, 0)
    const = lambda b, p, s: (0, 0)
    return pl.pallas_call(
        _ssd_kernel,
        grid=(BATCH, 2, N_CHUNKS),
        in_specs=[
            pl.BlockSpec((SSD_CHUNK, D_CONV), rows),
            pl.BlockSpec((SSD_CHUNK, DT_PAD), rows),
            pl.BlockSpec((DT_PAD, SSD_CHUNK), cols),
            pl.BlockSpec((SSD_CHUNK, D_SSM), fwd_only),
            pl.BlockSpec((1, DT_PAD), const),
            pl.BlockSpec((1, DT_PAD), const),
            pl.BlockSpec((DT_PAD, 1), const),
            pl.BlockSpec((DT_PAD, 1), const),
            pl.BlockSpec((1, D_SSM), const),
            pl.BlockSpec((1, D_SSM), const),
        ],
        out_specs=pl.BlockSpec((SSD_CHUNK, D_SSM), fwd_only),
        out_shape=jax.ShapeDtypeStruct((TOKENS, D_SSM), BF16),
        scratch_shapes=[
            pltpu.VMEM((N_BC_GROUPS, D_STATE, GROUP_WIDTH), F32),
            pltpu.VMEM((SSD_CHUNK, D_SSM), F32),
            pltpu.VMEM((N_CHUNKS, SSD_CHUNK, D_SSM), BF16),
        ],
        compiler_params=_cparams(("arbitrary", "arbitrary", "arbitrary")),
        name="ssd_scan",
    )(xconv, dt, dtT, z, alog_row, bias_row, alog_col, bias_col, dskip_row, norm_g)


def _outproj_kernel(x_ref, yf_ref, ys_ref, wf_ref, ws_ref, gate_ref, g_ref, shift_ref, scale_ref, x1_ref, h2_ref):
    mix = jnp.dot(yf_ref[...], wf_ref[...], preferred_element_type=F32)
    mix = mix + jnp.dot(ys_ref[...], ws_ref[...], preferred_element_type=F32)
    x1 = x_ref[...] + gate_ref[0] * mix
    x1_ref[...] = x1
    h2_ref[...] = _modulated_norm(x1, g_ref[...], shift_ref[0], scale_ref[0]).astype(BF16)


def _outproj(x2d, yf, ys, wf, ws, gate, gain, shift, scale):
    tiles_per_batch = SEQ // OUT_TM
    row = lambda i: (i, 0)
    const = lambda i: (0, 0)
    per_batch = lambda i: (i // tiles_per_batch, 0, 0)
    return pl.pallas_call(
        _outproj_kernel,
        grid=(TOKENS // OUT_TM,),
        in_specs=[
            pl.BlockSpec((OUT_TM, D_MODEL), row),
            pl.BlockSpec((OUT_TM, D_FNET), row),
            pl.BlockSpec((OUT_TM, D_SSM), row),
            pl.BlockSpec(wf.shape, const),
            pl.BlockSpec(ws.shape, const),
            pl.BlockSpec((1, 1, D_MODEL), per_batch),
            pl.BlockSpec((1, D_MODEL), const),
            pl.BlockSpec((1, 1, D_MODEL), per_batch),
            pl.BlockSpec((1, 1, D_MODEL), per_batch),
        ],
        out_specs=[pl.BlockSpec((OUT_TM, D_MODEL), row), pl.BlockSpec((OUT_TM, D_MODEL), row)],
        out_shape=[jax.ShapeDtypeStruct((TOKENS, D_MODEL), F32), jax.ShapeDtypeStruct((TOKENS, D_MODEL), BF16)],
        compiler_params=_cparams(("arbitrary",)),
        name="outproj",
    )(x2d, yf, ys, wf, ws, gate, gain, shift, scale)


def _extract_topk(x, k):
    rows = x.shape[0]
    iota = lax.broadcasted_iota(jnp.int32, x.shape, 0).astype(F32)
    rank = jnp.full(x.shape, float(k), F32)
    work = x
    vals = []
    for r in range(k):
        m = jnp.max(work, axis=0, keepdims=True)
        first = jnp.min(jnp.where(work == m, iota, float(rows)), axis=0, keepdims=True)
        sel = iota == first
        rank = jnp.where(sel, float(r), rank)
        work = jnp.where(sel, -jnp.inf, work)
        vals.append(m)
    return jnp.concatenate(vals, axis=0), rank


def _plan_kernel(h2_ref, wq_ref, keys_ref, c2_ref, e2_ref, n1_ref, e1_ref, q_ref):
    q_ref[...] = jnp.dot(h2_ref[...], wq_ref[...], preferred_element_type=F32).astype(BF16)

    def per_head(h, carry):
        def scores(side):
            col = pl.multiple_of((2 * h + side) * PEER_HALF, PEER_HALF)
            qh = q_ref[:, pl.ds(col, PEER_HALF)]
            return lax.dot_general(keys_ref[2 * h + side], qh, (((1,), (1,)), ((), ())), preferred_element_type=F32)

        s1 = scores(0)
        s2 = scores(1)
        v1, r1 = _extract_topk(s1, PEER_TOPK)
        v2, r2 = _extract_topk(s2, PEER_TOPK)
        cand = jnp.concatenate([v1[r:r + 1] + v2 for r in range(PEER_TOPK)], axis=0)
        _, crank = _extract_topk(cand, PEER_TOPK)
        chosen = (crank < float(PEER_TOPK)).astype(F32)
        top = v1[0:1] + v2[0:1]
        zsum = jnp.sum(chosen * jnp.exp(cand - top), axis=0, keepdims=True)
        n1 = jnp.zeros(s1.shape, F32)
        for r in range(PEER_TOPK):
            n_r = jnp.sum(chosen[r * PEER_TOPK:(r + 1) * PEER_TOPK], axis=0, keepdims=True)
            n1 = jnp.where(r1 == float(r), n_r, n1)
        c2_ref[h] = r2.astype(BF16)
        e2_ref[h] = jnp.exp(s2 - v2[0:1]).astype(BF16)
        n1_ref[h] = n1
        e1_ref[h] = jnp.exp(s1 - v1[0:1]) / zsum
        return carry

    lax.fori_loop(0, PEER_HEADS, per_head, 0)


def _plan(h2, wq, keys):
    blk = lambda i: (0, 0, i)
    shp = (PEER_HEADS, PEER_KEYS, TOKENS)
    return pl.pallas_call(
        _plan_kernel,
        grid=(TOKENS // PLAN_TT,),
        in_specs=[
            pl.BlockSpec((PLAN_TT, D_MODEL), lambda i: (i, 0)),
            pl.BlockSpec(wq.shape, lambda i: (0, 0)),
            pl.BlockSpec(keys.shape, lambda i: (0, 0, 0)),
        ],
        out_specs=[pl.BlockSpec((PEER_HEADS, PEER_KEYS, PLAN_TT), blk)] * 4,
        out_shape=[
            jax.ShapeDtypeStruct(shp, BF16),
            jax.ShapeDtypeStruct(shp, BF16),
            jax.ShapeDtypeStruct(shp, F32),
            jax.ShapeDtypeStruct(shp, F32),
        ],
        scratch_shapes=[pltpu.VMEM((PLAN_TT, 2 * PEER_HEADS * PEER_HALF), BF16)],
        compiler_params=_cparams(("arbitrary",)),
        name="peer_plan",
    )(h2, wq, keys)


def _gelu(x):
    return 0.5 * x * (1.0 + lax.erf(x * (1.0 / math.sqrt(2.0))))


def _dense_kernel(h2_ref, down_ref, upT_ref, c2_ref, e2_ref, n1_ref, e1_ref, x1_ref, gate_ref, g_ref,
                  o_ref, acc_ref, act_ref, w_ref):
    e = pl.program_id(1)

    @pl.when(e == 0)
    def _():
        acc_ref[...] = jnp.zeros(acc_ref.shape, F32)

    act_ref[...] = lax.dot_general(down_ref[...], h2_ref[...], (((1,), (1,)), ((), ())), preferred_element_type=F32)

    def per_key(k, carry):
        i1 = e * (DENSE_TE // PEER_KEYS) + k
        row0 = pl.multiple_of(k * PEER_KEYS, PEER_KEYS)
        gates = jnp.zeros((PEER_KEYS, DENSE_TT), BF16)
        for h in range(PEER_HEADS):
            n_row = n1_ref[h, pl.ds(i1, 1), :].astype(BF16)
            e_row = e1_ref[h, pl.ds(i1, 1), :].astype(BF16)
            gates = gates + jnp.where(c2_ref[h] < n_row, e2_ref[h], jnp.zeros((), BF16)) * e_row
        act = act_ref[pl.ds(row0, PEER_KEYS), :]
        w_ref[pl.ds(row0, PEER_KEYS), :] = gates * _gelu(act).astype(BF16)
        return carry

    lax.fori_loop(0, DENSE_TE // PEER_KEYS, per_key, 0)
    acc_ref[...] += jnp.dot(upT_ref[...], w_ref[...], preferred_element_type=F32)

    @pl.when(e == pl.num_programs(1) - 1)
    def _():
        x2 = x1_ref[...] + gate_ref[0] * acc_ref[...].T
        ms = jnp.mean(x2 * x2, axis=-1, keepdims=True)
        o_ref[...] = x2 * lax.rsqrt(ms + EPS) * g_ref[...]


def _dense(h2, down, upT, c2, e2, n1, e1, x1, gate, final_g):
    tiles_per_batch = SEQ // DENSE_TT
    tok = lambda i, e: (i, 0)
    plan = lambda i, e: (0, 0, i)
    plan_blk = (PEER_HEADS, PEER_KEYS, DENSE_TT)
    return pl.pallas_call(
        _dense_kernel,
        grid=(TOKENS // DENSE_TT, PEER_EXPERTS // DENSE_TE),
        in_specs=[
            pl.BlockSpec((DENSE_TT, D_MODEL), tok),
            pl.BlockSpec((DENSE_TE, D_MODEL), lambda i, e: (e, 0)),
            pl.BlockSpec((D_MODEL, DENSE_TE), lambda i, e: (0, e)),
            pl.BlockSpec(plan_blk, plan),
            pl.BlockSpec(plan_blk, plan),
            pl.BlockSpec(plan_blk, plan),
            pl.BlockSpec(plan_blk, plan),
            pl.BlockSpec((DENSE_TT, D_MODEL), tok),
            pl.BlockSpec((1, 1, D_MODEL), lambda i, e: (i // tiles_per_batch, 0, 0)),
            pl.BlockSpec((1, D_MODEL), lambda i, e: (0, 0)),
        ],
        out_specs=pl.BlockSpec((DENSE_TT, D_MODEL), tok),
        out_shape=jax.ShapeDtypeStruct((TOKENS, D_MODEL), F32),
        scratch_shapes=[
            pltpu.VMEM((D_MODEL, DENSE_TT), F32),
            pltpu.VMEM((DENSE_TE, DENSE_TT), F32),
            pltpu.VMEM((DENSE_TE, DENSE_TT), BF16),
        ],
        compiler_params=_cparams(("arbitrary", "arbitrary")),
        name="peer_dense",
    )(h2, down, upT, c2, e2, n1, e1, x1, gate, final_g)


def _transpose_kernel(x_ref, o_ref):
    o_ref[...] = x_ref[...].T.astype(BF16)


def _transpose_table(t):
    rows, cols = t.shape
    tr = 512
    return pl.pallas_call(
        _transpose_kernel,
        grid=(rows // tr,),
        in_specs=[pl.BlockSpec((tr, cols), lambda i: (i, 0))],
        out_specs=pl.BlockSpec((cols, tr), lambda i: (0, i)),
        out_shape=jax.ShapeDtypeStruct((cols, rows), BF16),
        compiler_params=_cparams(("arbitrary",)),
        name="table_transpose",
    )(t)


def _position_dft_tables():
    j = lax.broadcasted_iota(jnp.int32, (SEQ, SEQ), 0)
    k = lax.broadcasted_iota(jnp.int32, (SEQ, SEQ), 1)
    ang = ((j * k) % SEQ).astype(F32) * (2.0 * math.pi / SEQ)
    scale = 1.0 / math.sqrt(SEQ)
    return (jnp.cos(ang) * scale).astype(BF16), (jnp.sin(ang) * scale).astype(BF16)


def _channel_dft_table():
    j = lax.broadcasted_iota(jnp.int32, (FNET_GROUP, FNET_GROUP), 0)
    k = lax.broadcasted_iota(jnp.int32, (FNET_GROUP, FNET_GROUP), 1)
    ang = ((j * k) % FNET_GROUP).astype(F32) * (2.0 * math.pi / FNET_GROUP)
    scale = 1.0 / math.sqrt(FNET_GROUP)
    return jnp.concatenate([jnp.cos(ang) * scale, jnp.sin(ang) * scale], axis=1).astype(BF16)


def _pad_lanes(v, width):
    return jnp.pad(v, ((0, 0), (0, width - v.shape[1])))


def kernel(x, c, w_ada, b_ada, norm_mix_g, w_in, conv_w, conv_b, a_log_fwd, a_log_bwd, dt_bias_fwd, dt_bias_bwd, d_skip, ssm_norm_g, w_out, norm_ffn_g, w_query, sub_keys, expert_down, expert_up, final_norm_g):
    assert w_ada.shape[0] == 1, "single-layer problem: the final RMSNorm is fused into the PEER kernel"
    xt = x.reshape(TOKENS, D_MODEL)
    c_pad = jnp.pad(c, ((0, SUBLANES - BATCH), (0, 0)))
    cs_tab, ss_tab = _position_dft_tables()
    csc_tab = _channel_dft_table()
    for layer in range(1):
        mod = _adaln(c_pad, w_ada[layer], b_ada[layer][None, :])[:BATCH]
        shift_m, scale_m, gate_m, shift_f, scale_f, gate_f = [
            m.reshape(BATCH, 1, D_MODEL) for m in jnp.split(mod, 6, axis=-1)]

        w = w_in[layer]
        wf = w[:, :D_FNET].astype(BF16)
        wz = w[:, D_FNET:D_FNET + D_SSM].astype(BF16)
        wx = w[:, D_FNET + D_SSM:D_FNET + D_SSM + D_CONV].astype(BF16)
        wdt = _pad_lanes(w[:, D_FNET + D_SSM + D_CONV:], DT_PAD).astype(BF16)
        uc, us, z, xbc, dt, dtT = _inproj(xt, norm_mix_g[layer][None, :], shift_m, scale_m,
                                          wf, wz, wx, wdt, wdt.T, csc_tab)
        y_fnet = _dft(cs_tab, ss_tab, uc, us)

        xconv = _conv(xbc.reshape(BATCH, SEQ, D_CONV), conv_w[layer], conv_b[layer][None, :])
        alog_row = _pad_lanes(jnp.concatenate([a_log_fwd[layer], a_log_bwd[layer]])[None, :], DT_PAD)
        bias_row = _pad_lanes(jnp.concatenate([dt_bias_fwd[layer], dt_bias_bwd[layer]])[None, :], DT_PAD)
        dskip_row = jnp.repeat(d_skip[layer], SSM_HEAD_DIM)[None, :]
        y_ssm = _ssd(xconv.reshape(TOKENS, D_CONV), dt, dtT, z, alog_row, bias_row, alog_row.T, bias_row.T,
                     dskip_row, ssm_norm_g[layer][None, :])

        wo = w_out[layer].astype(BF16)
        x1, h2 = _outproj(xt, y_fnet, y_ssm, wo[:D_FNET], wo[D_FNET:], gate_m,
                          norm_ffn_g[layer][None, :], shift_f, scale_f)

        keys = sub_keys[layer].reshape(2 * PEER_HEADS, PEER_KEYS, PEER_HALF).astype(BF16)
        c2, e2, n1, e1 = _plan(h2, w_query[layer].astype(BF16), keys)
        xt = _dense(h2, expert_down[layer].astype(BF16), _transpose_table(expert_up[layer]),
                    c2, e2, n1, e1, x1, gate_f, final_norm_g[None, :])
    return xt.reshape(BATCH, SEQ, D_MODEL)
```

```python
import functools
import math

import jax
import jax.numpy as jnp
from jax import lax
from jax.experimental import pallas as pl
from jax.experimental.pallas import tpu as pltpu

F32 = jnp.float32
BF16 = jnp.bfloat16
HIGHEST = lax.Precision.HIGHEST

D_MODEL = 1024
BATCH = 4
SEQ = 4096
TOKENS = BATCH * SEQ
D_MIX = 2 * D_MODEL
D_FNET = D_MIX // 4
N_FNET_GROUPS = 4
FNET_GROUP = D_FNET // N_FNET_GROUPS
D_SSM = D_MIX - D_FNET
SSM_HEAD_DIM = 64
N_SSM_HEADS = D_SSM // SSM_HEAD_DIM
N_BC_GROUPS = 4
HEADS_PER_GROUP = N_SSM_HEADS // N_BC_GROUPS
D_STATE = 128
CONV_WIDTH = 5
SSD_CHUNK = 128
N_CHUNKS = SEQ // SSD_CHUNK
D_BC = N_BC_GROUPS * D_STATE
D_CONV = D_SSM + 2 * D_BC
GROUP_WIDTH = D_SSM // N_BC_GROUPS
PEER_HEADS = 8
PEER_KEYS = 128
PEER_EXPERTS = PEER_KEYS * PEER_KEYS
PEER_HALF = 128
PEER_TOPK = 16
EPS = 1e-6

LANES = 128
SUBLANES = 8
DT_PAD = LANES
VMEM_LIMIT = 56 * 1024 * 1024

ADA_TN = 512
PROJ_TM = 512
DFT_TM = 512
CONV_TC = 256
CONV_TR = 512
CONV_HALO = SUBLANES
OUT_TM = 512
PLAN_TT = 256
DENSE_TT = 512
DENSE_TE = 1024
DENSE_SUB = 256


def _cparams(sem):
    return pltpu.CompilerParams(dimension_semantics=sem, vmem_limit_bytes=VMEM_LIMIT)


def _sigmoid(x):
    return 1.0 / (1.0 + jnp.exp(-x))


def _softplus(x):
    return jnp.maximum(x, 0.0) + jnp.log(1.0 + jnp.exp(-jnp.abs(x)))


def _adaln_kernel(c_ref, w_ref, b_ref, o_ref):
    c = c_ref[...]
    ca = c * _sigmoid(c)
    o_ref[...] = jnp.dot(ca, w_ref[...], preferred_element_type=F32, precision=HIGHEST) + b_ref[...]


def _adaln(c_pad, w_ada, b_ada):
    n = w_ada.shape[1]
    return pl.pallas_call(
        _adaln_kernel,
        grid=(n // ADA_TN,),
        in_specs=[
            pl.BlockSpec((SUBLANES, D_MODEL), lambda j: (0, 0)),
            pl.BlockSpec((D_MODEL, ADA_TN), lambda j: (0, j)),
            pl.BlockSpec((1, ADA_TN), lambda j: (0, j)),
        ],
        out_specs=pl.BlockSpec((SUBLANES, ADA_TN), lambda j: (0, j)),
        out_shape=jax.ShapeDtypeStruct((SUBLANES, n), F32),
        compiler_params=_cparams(("arbitrary",)),
        name="adaln",
    )(c_pad, w_ada, b_ada)


def _modulated_norm(x, gain, shift, scale):
    ms = jnp.mean(x * x, axis=-1, keepdims=True)
    xn = x * lax.rsqrt(ms + EPS) * gain
    return xn * (1.0 + scale) + shift


def _inproj_kernel(x_ref, g_ref, shift_ref, scale_ref, wf_ref, wz_ref, wx_ref, wdt_ref, wdtT_ref, csc_ref,
                   uc_ref, us_ref, z_ref, xbc_ref, dt_ref, dtT_ref):
    h = _modulated_norm(x_ref[...], g_ref[...], shift_ref[0], scale_ref[0])
    hb = h.astype(BF16)
    fb = jnp.dot(hb, wf_ref[...], preferred_element_type=F32).astype(BF16)
    for g in range(N_FNET_GROUPS):
        sl = slice(g * FNET_GROUP, (g + 1) * FNET_GROUP)
        u = jnp.dot(fb[:, sl], csc_ref[...], preferred_element_type=F32)
        uc_ref[:, sl] = u[:, :FNET_GROUP].astype(BF16)
        us_ref[:, sl] = u[:, FNET_GROUP:].astype(BF16)
    z_ref[...] = jnp.dot(hb, wz_ref[...], preferred_element_type=F32).astype(BF16)
    xbc_ref[...] = jnp.dot(hb, wx_ref[...], preferred_element_type=F32).astype(BF16)
    dt_ref[...] = jnp.dot(hb, wdt_ref[...], preferred_element_type=F32)
    dtT_ref[...] = lax.dot_general(wdtT_ref[...], hb, (((1,), (1,)), ((), ())), preferred_element_type=F32)


def _inproj(x2d, gain, shift, scale, wf, wz, wx, wdt, wdtT, csc):
    tiles_per_batch = SEQ // PROJ_TM
    row = lambda i: (i, 0)
    const = lambda i: (0, 0)
    per_batch = lambda i: (i // tiles_per_batch, 0, 0)
    return pl.pallas_call(
        _inproj_kernel,
        grid=(TOKENS // PROJ_TM,),
        in_specs=[
            pl.BlockSpec((PROJ_TM, D_MODEL), row),
            pl.BlockSpec((1, D_MODEL), const),
            pl.BlockSpec((1, 1, D_MODEL), per_batch),
            pl.BlockSpec((1, 1, D_MODEL), per_batch),
            pl.BlockSpec(wf.shape, const),
            pl.BlockSpec(wz.shape, const),
            pl.BlockSpec(wx.shape, const),
            pl.BlockSpec(wdt.shape, const),
            pl.BlockSpec(wdtT.shape, const),
            pl.BlockSpec(csc.shape, const),
        ],
        out_specs=[
            pl.BlockSpec((PROJ_TM, D_FNET), row),
            pl.BlockSpec((PROJ_TM, D_FNET), row),
            pl.BlockSpec((PROJ_TM, D_SSM), row),
            pl.BlockSpec((PROJ_TM, D_CONV), row),
            pl.BlockSpec((PROJ_TM, DT_PAD), row),
            pl.BlockSpec((DT_PAD, PROJ_TM), lambda i: (0, i)),
        ],
        out_shape=[
            jax.ShapeDtypeStruct((TOKENS, D_FNET), BF16),
            jax.ShapeDtypeStruct((TOKENS, D_FNET), BF16),
            jax.ShapeDtypeStruct((TOKENS, D_SSM), BF16),
            jax.ShapeDtypeStruct((TOKENS, D_CONV), BF16),
            jax.ShapeDtypeStruct((TOKENS, DT_PAD), F32),
            jax.ShapeDtypeStruct((DT_PAD, TOKENS), F32),
        ],
        compiler_params=_cparams(("arbitrary",)),
        name="inproj",
    )(x2d, gain, shift, scale, wf, wz, wx, wdt, wdtT, csc)


def _dft_kernel(cs_ref, ss_ref, uc_ref, us_ref, o_ref):
    y = jnp.dot(cs_ref[...], uc_ref[...], preferred_element_type=F32)
    y = y - jnp.dot(ss_ref[...], us_ref[...], preferred_element_type=F32)
    o_ref[...] = y.astype(BF16)


def _dft(cs, ss, uc, us):
    mt = SEQ // DFT_TM
    return pl.pallas_call(
        _dft_kernel,
        grid=(BATCH, mt),
        in_specs=[
            pl.BlockSpec((DFT_TM, SEQ), lambda b, m: (m, 0)),
            pl.BlockSpec((DFT_TM, SEQ), lambda b, m: (m, 0)),
            pl.BlockSpec((SEQ, D_FNET), lambda b, m: (b, 0)),
            pl.BlockSpec((SEQ, D_FNET), lambda b, m: (b, 0)),
        ],
        out_specs=pl.BlockSpec((DFT_TM, D_FNET), lambda b, m: (b * mt + m, 0)),
        out_shape=jax.ShapeDtypeStruct((TOKENS, D_FNET), BF16),
        compiler_params=_cparams(("arbitrary", "arbitrary")),
        name="fnet_dft",
    )(cs, ss, uc, us)


def _conv_kernel(x_ref, w_ref, b_ref, o_ref, pad_ref):
    zeros = jnp.zeros((CONV_HALO, CONV_TC), F32)
    pad_ref[0:CONV_HALO, :] = zeros
    pad_ref[SEQ + CONV_HALO:SEQ + 2 * CONV_HALO, :] = zeros
    pad_ref[CONV_HALO:SEQ + CONV_HALO, :] = x_ref[0].astype(F32)
    w = w_ref[...]
    bias = b_ref[...]
    first = CONV_HALO - CONV_WIDTH // 2
    for i in range(SEQ // CONV_TR):
        acc = jnp.broadcast_to(bias, (CONV_TR, CONV_TC))
        for k in range(CONV_WIDTH):
            lo = first + k + i * CONV_TR
            acc = acc + pad_ref[lo:lo + CONV_TR, :] * w[k:k + 1, :]
        o_ref[0, i * CONV_TR:(i + 1) * CONV_TR, :] = (acc * _sigmoid(acc)).astype(BF16)


def _conv(xbc3, conv_w, conv_b):
    return pl.pallas_call(
        _conv_kernel,
        grid=(BATCH, D_CONV // CONV_TC),
        in_specs=[
            pl.BlockSpec((1, SEQ, CONV_TC), lambda b, j: (b, 0, j)),
            pl.BlockSpec((CONV_WIDTH, CONV_TC), lambda b, j: (0, j)),
            pl.BlockSpec((1, CONV_TC), lambda b, j: (0, j)),
        ],
        out_specs=pl.BlockSpec((1, SEQ, CONV_TC), lambda b, j: (b, 0, j)),
        out_shape=jax.ShapeDtypeStruct((BATCH, SEQ, D_CONV), BF16),
        scratch_shapes=[pltpu.VMEM((SEQ + 2 * CONV_HALO, CONV_TC), F32)],
        compiler_params=_cparams(("arbitrary", "arbitrary")),
        name="conv_silu",
    )(xbc3, conv_w, conv_b)


def _ssd_direction(xc, dtvT, cs, csT, a, aT, decay_row, state_ref, yacc_ref, head_base, forward):
    li = lax.broadcasted_iota(jnp.int32, (SSD_CHUNK, SSD_CHUNK), 0)
    si = lax.broadcasted_iota(jnp.int32, (SSD_CHUNK, SSD_CHUNK), 1)
    low_half = si < SSM_HEAD_DIM
    if forward:
        u, uT = cs, csT
        mask = li >= si
    else:
        u, uT = cs - a, csT - aT
        mask = si >= li
    pair = 2 * SSM_HEAD_DIM
    for g in range(N_BC_GROUPS):
        bc = xc[:, D_SSM + g * D_STATE:D_SSM + (g + 1) * D_STATE]
        cc = xc[:, D_SSM + D_BC + g * D_STATE:D_SSM + D_BC + (g + 1) * D_STATE]
        gmat = lax.dot_general(cc, bc, (((1,), (1,)), ((), ())), preferred_element_type=F32)
        bcT = bc.astype(F32).T
        ccf = cc.astype(F32)
        for pr in range(HEADS_PER_GROUP // 2):
            h0 = g * HEADS_PER_GROUP + 2 * pr
            xcols = slice(h0 * SSM_HEAD_DIM, h0 * SSM_HEAD_DIM + pair)
            scols = slice(pr * pair, (pr + 1) * pair)
            xpair = xc[:, xcols]
            spair = state_ref[g, :, scols]
            rhs = jnp.concatenate([xpair, spair.astype(BF16)], axis=0)
            ys, upds = [], []
            for j in range(2):
                hh = head_base + h0 + j
                ub = jnp.broadcast_to(u[:, hh:hh + 1], (SSD_CHUNK, SSD_CHUNK))
                urow = uT[hh:hh + 1, :]
                dtrow = dtvT[hh:hh + 1, :]
                tot = cs[SSD_CHUNK - 1:SSD_CHUNK, hh:hh + 1]
                if forward:
                    diff = ub - urow
                    cscale = jnp.exp(ub)
                    wrow = jnp.exp(tot - urow) * dtrow
                else:
                    diff = urow - ub
                    cscale = jnp.exp(tot - ub)
                    wrow = jnp.exp(urow) * dtrow
                lmat = jnp.where(mask, jnp.exp(jnp.minimum(diff, 0.0)), 0.0)
                lhs = jnp.concatenate([(gmat * lmat * dtrow).astype(BF16), (ccf * cscale).astype(BF16)], axis=1)
                ys.append(jnp.dot(lhs, rhs, preferred_element_type=F32))
                upds.append(jnp.dot((bcT * wrow).astype(BF16), xpair, preferred_element_type=F32))
            yacc_ref[:, xcols] = jnp.where(low_half, ys[0], ys[1])
            state_ref[g, :, scols] = spair * decay_row[:, xcols] + jnp.where(low_half, upds[0], upds[1])


def _ssd_kernel(xc_ref, dt_ref, dtT_ref, z_ref, alog_row_ref, bias_row_ref, alog_col_ref, bias_col_ref,
                dskip_ref, ng_ref, expand_ref, y_ref, state_ref, yacc_ref, yb_ref):
    phase = pl.program_id(1)
    step = pl.program_id(2)

    @pl.when(step == 0)
    def _():
        state_ref[...] = jnp.zeros(state_ref.shape, F32)

    xc = xc_ref[...]
    dtv = _softplus(dt_ref[...] + bias_row_ref[...])
    a = dtv * (-jnp.exp(alog_row_ref[...]))
    dtvT = _softplus(dtT_ref[...] + bias_col_ref[...])
    aT = dtvT * (-jnp.exp(alog_col_ref[...]))
    li = lax.broadcasted_iota(jnp.int32, (SSD_CHUNK, SSD_CHUNK), 0)
    si = lax.broadcasted_iota(jnp.int32, (SSD_CHUNK, SSD_CHUNK), 1)
    lower = (li >= si).astype(F32)
    upper = (li <= si).astype(F32)
    cs = jnp.dot(lower, a, preferred_element_type=F32, precision=HIGHEST)
    csT = jnp.dot(aT, upper, preferred_element_type=F32, precision=HIGHEST)
    chunk_decay = jnp.exp(cs[SSD_CHUNK - 1:SSD_CHUNK, :])

    def decay_row(direction):
        return jnp.dot(chunk_decay, expand_ref[direction], preferred_element_type=F32, precision=HIGHEST)

    @pl.when(phase == 0)
    def _():
        chunk = N_CHUNKS - 1 - step
        _ssd_direction(xc, dtvT, cs, csT, a, aT, decay_row(1), state_ref, yacc_ref, N_SSM_HEADS, False)
        yb_ref[chunk] = yacc_ref[...].astype(BF16)

    @pl.when(phase == 1)
    def _():
        _ssd_direction(xc, dtvT, cs, csT, a, aT, decay_row(0), state_ref, yacc_ref, 0, True)
        xs = xc[:, :D_SSM].astype(F32)
        y = yacc_ref[...] + yb_ref[step].astype(F32) + xs * dskip_ref[...]
        z = z_ref[...].astype(F32)
        y = y * (z * _sigmoid(z))
        for g in range(N_BC_GROUPS):
            gs = slice(g * GROUP_WIDTH, (g + 1) * GROUP_WIDTH)
            yg = y[:, gs]
            ms = jnp.mean(yg * yg, axis=-1, keepdims=True)
            y_ref[:, gs] = (yg * lax.rsqrt(ms + EPS) * ng_ref[:, gs]).astype(BF16)


def _head_expansion():
    col = lax.broadcasted_iota(jnp.int32, (2, DT_PAD, D_SSM), 1)
    head = lax.broadcasted_iota(jnp.int32, (2, DT_PAD, D_SSM), 2) // SSM_HEAD_DIM
    direction = lax.broadcasted_iota(jnp.int32, (2, DT_PAD, D_SSM), 0)
    return (col == direction * N_SSM_HEADS + head).astype(F32)


def _ssd(xconv, dt, dtT, z, alog_row, bias_row, alog_col, bias_col, dskip_row, norm_g):
    def chunk_of(p, s):
        return p * s + (1 - p) * (N_CHUNKS - 1 - s)

    rows = lambda b, p, s: (b * N_CHUNKS + chunk_of(p, s), 0)
    cols = lambda b, p, s: (0, b * N_CHUNKS + chunk_of(p, s))
    fwd_only = lambda b, p, s: (b * N_CHUNKS + p * s, 0)
    const = lambda b, p, s: (0, 0)
    return pl.pallas_call(
        _ssd_kernel,
        grid=(BATCH, 2, N_CHUNKS),
        in_specs=[
            pl.BlockSpec((SSD_CHUNK, D_CONV), rows),
            pl.BlockSpec((SSD_CHUNK, DT_PAD), rows),
            pl.BlockSpec((DT_PAD, SSD_CHUNK), cols),
            pl.BlockSpec((SSD_CHUNK, D_SSM), fwd_only),
            pl.BlockSpec((1, DT_PAD), const),
            pl.BlockSpec((1, DT_PAD), const),
            pl.BlockSpec((DT_PAD, 1), const),
            pl.BlockSpec((DT_PAD, 1), const),
            pl.BlockSpec((1, D_SSM), const),
            pl.BlockSpec((1, D_SSM), const),
            pl.BlockSpec((2, DT_PAD, D_SSM), lambda b, p, s: (0, 0, 0)),
        ],
        out_specs=pl.BlockSpec((SSD_CHUNK, D_SSM), fwd_only),
        out_shape=jax.ShapeDtypeStruct((TOKENS, D_SSM), BF16),
        scratch_shapes=[
            pltpu.VMEM((N_BC_GROUPS, D_STATE, GROUP_WIDTH), F32),
            pltpu.VMEM((SSD_CHUNK, D_SSM), F32),
            pltpu.VMEM((N_CHUNKS, SSD_CHUNK, D_SSM), BF16),
        ],
        compiler_params=_cparams(("arbitrary", "arbitrary", "arbitrary")),
        name="ssd_scan",
    )(xconv, dt, dtT, z, alog_row, bias_row, alog_col, bias_col, dskip_row, norm_g, _head_expansion())


def _outproj_kernel(x_ref, yf_ref, ys_ref, wf_ref, ws_ref, gate_ref, g_ref, shift_ref, scale_ref, x1_ref, h2_ref):
    mix = jnp.dot(yf_ref[...], wf_ref[...], preferred_element_type=F32)
    mix = mix + jnp.dot(ys_ref[...], ws_ref[...], preferred_element_type=F32)
    x1 = x_ref[...] + gate_ref[0] * mix
    x1_ref[...] = x1
    h2_ref[...] = _modulated_norm(x1, g_ref[...], shift_ref[0], scale_ref[0]).astype(BF16)


def _outproj(x2d, yf, ys, wf, ws, gate, gain, shift, scale):
    tiles_per_batch = SEQ // OUT_TM
    row = lambda i: (i, 0)
    const = lambda i: (0, 0)
    per_batch = lambda i: (i // tiles_per_batch, 0, 0)
    return pl.pallas_call(
        _outproj_kernel,
        grid=(TOKENS // OUT_TM,),
        in_specs=[
            pl.BlockSpec((OUT_TM, D_MODEL), row),
            pl.BlockSpec((OUT_TM, D_FNET), row),
            pl.BlockSpec((OUT_TM, D_SSM), row),
            pl.BlockSpec(wf.shape, const),
            pl.BlockSpec(ws.shape, const),
            pl.BlockSpec((1, 1, D_MODEL), per_batch),
            pl.BlockSpec((1, D_MODEL), const),
            pl.BlockSpec((1, 1, D_MODEL), per_batch),
            pl.BlockSpec((1, 1, D_MODEL), per_batch),
        ],
        out_specs=[pl.BlockSpec((OUT_TM, D_MODEL), row), pl.BlockSpec((OUT_TM, D_MODEL), row)],
        out_shape=[jax.ShapeDtypeStruct((TOKENS, D_MODEL), F32), jax.ShapeDtypeStruct((TOKENS, D_MODEL), BF16)],
        compiler_params=_cparams(("arbitrary",)),
        name="outproj",
    )(x2d, yf, ys, wf, ws, gate, gain, shift, scale)


def _extract_topk(x, k):
    rows = x.shape[0]
    iota = lax.broadcasted_iota(jnp.int32, x.shape, 0).astype(F32)
    rank = jnp.full(x.shape, float(k), F32)
    work = x
    vals = []
    for r in range(k):
        m = jnp.max(work, axis=0, keepdims=True)
        first = jnp.min(jnp.where(work == m, iota, float(rows)), axis=0, keepdims=True)
        sel = iota == first
        rank = jnp.where(sel, float(r), rank)
        work = jnp.where(sel, -jnp.inf, work)
        vals.append(m)
    return jnp.concatenate(vals, axis=0), rank


def _plan_kernel(h2_ref, wq_ref, keys_ref, c2_ref, e2_ref, n1_ref, e1_ref, q_ref):
    q_ref[...] = jnp.dot(h2_ref[...], wq_ref[...], preferred_element_type=F32).astype(BF16)

    def per_head(h, carry):
        def scores(side):
            col = pl.multiple_of((2 * h + side) * PEER_HALF, PEER_HALF)
            qh = q_ref[:, pl.ds(col, PEER_HALF)]
            return lax.dot_general(keys_ref[2 * h + side], qh, (((1,), (1,)), ((), ())), preferred_element_type=F32)

        s1 = scores(0)
        s2 = scores(1)
        v1, r1 = _extract_topk(s1, PEER_TOPK)
        v2, r2 = _extract_topk(s2, PEER_TOPK)
        cand = jnp.concatenate([v1[r:r + 1] + v2 for r in range(PEER_TOPK)], axis=0)
        _, crank = _extract_topk(cand, PEER_TOPK)
        chosen = (crank < float(PEER_TOPK)).astype(F32)
        top = v1[0:1] + v2[0:1]
        zsum = jnp.sum(chosen * jnp.exp(cand - top), axis=0, keepdims=True)
        n1 = jnp.zeros(s1.shape, F32)
        for r in range(PEER_TOPK):
            n_r = jnp.sum(chosen[r * PEER_TOPK:(r + 1) * PEER_TOPK], axis=0, keepdims=True)
            n1 = jnp.where(r1 == float(r), n_r, n1)
        c2_ref[h] = r2.astype(BF16)
        e2_ref[h] = jnp.exp(s2 - v2[0:1]).astype(BF16)
        n1_ref[h] = n1
        e1_ref[h] = jnp.exp(s1 - v1[0:1]) / zsum
        return carry

    lax.fori_loop(0, PEER_HEADS, per_head, 0)


def _plan(h2, wq, keys):
    blk = lambda i: (0, 0, i)
    shp = (PEER_HEADS, PEER_KEYS, TOKENS)
    return pl.pallas_call(
        _plan_kernel,
        grid=(TOKENS // PLAN_TT,),
        in_specs=[
            pl.BlockSpec((PLAN_TT, D_MODEL), lambda i: (i, 0)),
            pl.BlockSpec(wq.shape, lambda i: (0, 0)),
            pl.BlockSpec(keys.shape, lambda i: (0, 0, 0)),
        ],
        out_specs=[pl.BlockSpec((PEER_HEADS, PEER_KEYS, PLAN_TT), blk)] * 4,
        out_shape=[
            jax.ShapeDtypeStruct(shp, BF16),
            jax.ShapeDtypeStruct(shp, BF16),
            jax.ShapeDtypeStruct(shp, F32),
            jax.ShapeDtypeStruct(shp, F32),
        ],
        scratch_shapes=[pltpu.VMEM((PLAN_TT, 2 * PEER_HEADS * PEER_HALF), BF16)],
        compiler_params=_cparams(("arbitrary",)),
        name="peer_plan",
    )(h2, wq, keys)


def _gelu(x):
    return 0.5 * x * (1.0 + lax.erf(x * (1.0 / math.sqrt(2.0))))


DENSE_NE = PEER_EXPERTS // DENSE_TE
DENSE_NT = TOKENS // DENSE_TT
DENSE_TILES = DENSE_NT * DENSE_NE
DENSE_KEYS_PER_TILE = DENSE_TE // PEER_KEYS


def _dense_stage_body(f, h2_ref, down_ref, upT_ref, c2_ref, e2_ref, n1_ref, e1_ref, acc_ref,
                      act_w, act_r, w_w, w_r):
    gate_tile = jnp.clip(f - 1, 0, DENSE_TILES - 1) % DENSE_NE
    gate_valid = jnp.logical_and(f >= 1, f <= DENSE_TILES)

    def act_piece(q):
        rows = slice(q * DENSE_SUB, (q + 1) * DENSE_SUB)
        act_w[rows, :] = lax.dot_general(down_ref[rows, :], h2_ref[...], (((1,), (1,)), ((), ())),
                                         preferred_element_type=F32)

    def gate_piece(k):
        i1 = gate_tile * DENSE_KEYS_PER_TILE + k
        rows = slice(k * PEER_KEYS, (k + 1) * PEER_KEYS)
        gates = jnp.zeros((PEER_KEYS, DENSE_TT), BF16)
        for h in range(PEER_HEADS):
            n_row = n1_ref[h, pl.ds(i1, 1), :].astype(BF16)
            e_row = e1_ref[h, pl.ds(i1, 1), :].astype(BF16)
            gates = gates + jnp.where(c2_ref[h] < n_row, e2_ref[h], jnp.zeros((), BF16)) * e_row
        w = gates * _gelu(act_r[rows, :]).astype(BF16)
        w_w[rows, :] = jnp.where(gate_valid, w, jnp.zeros((), BF16))

    def out_piece(q):
        rows = slice(q * DENSE_SUB, (q + 1) * DENSE_SUB)
        acc_ref[rows, :] += jnp.dot(upT_ref[rows, :], w_r[...], preferred_element_type=F32)

    gates_per_sub = DENSE_SUB // PEER_KEYS
    for q in range(DENSE_TE // DENSE_SUB):
        act_piece(q)
        gate_piece(gates_per_sub * q)
        out_piece(q)
        for k in range(1, gates_per_sub):
            gate_piece(gates_per_sub * q + k)


def _dense_kernel(h2_ref, down_ref, upT_ref, c2_ref, e2_ref, n1_ref, e1_ref, x1_ref, gate_ref, g_ref,
                  o_ref, acc_ref, act0_ref, act1_ref, w0_ref, w1_ref):
    f = pl.program_id(0)
    out_tile = jnp.clip(f - 2, 0, DENSE_TILES - 1) % DENSE_NE

    @pl.when(f == 0)
    def _():
        act1_ref[...] = jnp.zeros(act1_ref.shape, F32)
        w0_ref[...] = jnp.zeros(w0_ref.shape, BF16)

    @pl.when(out_tile == 0)
    def _():
        acc_ref[...] = jnp.zeros(acc_ref.shape, F32)

    stage = functools.partial(_dense_stage_body, f, h2_ref, down_ref, upT_ref, c2_ref, e2_ref, n1_ref, e1_ref,
                              acc_ref)

    @pl.when(f % 2 == 0)
    def _():
        stage(act0_ref, act1_ref, w1_ref, w0_ref)

    @pl.when(f % 2 == 1)
    def _():
        stage(act1_ref, act0_ref, w0_ref, w1_ref)

    @pl.when(jnp.logical_and(out_tile == DENSE_NE - 1, f >= 2))
    def _():
        x2 = x1_ref[...] + gate_ref[0] * acc_ref[...].T
        ms = jnp.mean(x2 * x2, axis=-1, keepdims=True)
        o_ref[...] = x2 * lax.rsqrt(ms + EPS) * g_ref[...]


def _dense(h2, down, upT, c2, e2, n1, e1, x1, gate, final_g):
    tiles_per_batch = SEQ // DENSE_TT

    def tile(f, lag):
        return jnp.clip(f - lag, 0, DENSE_TILES - 1)

    plan = lambda f: (0, 0, tile(f, 1) // DENSE_NE)
    out_tok = lambda f: (tile(f, 2) // DENSE_NE, 0)
    plan_blk = (PEER_HEADS, PEER_KEYS, DENSE_TT)
    return pl.pallas_call(
        _dense_kernel,
        grid=(DENSE_TILES + 2,),
        in_specs=[
            pl.BlockSpec((DENSE_TT, D_MODEL), lambda f: (tile(f, 0) // DENSE_NE, 0)),
            pl.BlockSpec((DENSE_TE, D_MODEL), lambda f: (tile(f, 0) % DENSE_NE, 0)),
            pl.BlockSpec((D_MODEL, DENSE_TE), lambda f: (0, tile(f, 2) % DENSE_NE)),
            pl.BlockSpec(plan_blk, plan),
            pl.BlockSpec(plan_blk, plan),
            pl.BlockSpec(plan_blk, plan),
            pl.BlockSpec(plan_blk, plan),
            pl.BlockSpec((DENSE_TT, D_MODEL), out_tok),
            pl.BlockSpec((1, 1, D_MODEL), lambda f: (tile(f, 2) // DENSE_NE // tiles_per_batch, 0, 0)),
            pl.BlockSpec((1, D_MODEL), lambda f: (0, 0)),
        ],
        out_specs=pl.BlockSpec((DENSE_TT, D_MODEL), out_tok),
        out_shape=jax.ShapeDtypeStruct((TOKENS, D_MODEL), F32),
        scratch_shapes=[
            pltpu.VMEM((D_MODEL, DENSE_TT), F32),
            pltpu.VMEM((DENSE_TE, DENSE_TT), F32),
            pltpu.VMEM((DENSE_TE, DENSE_TT), F32),
            pltpu.VMEM((DENSE_TE, DENSE_TT), BF16),
            pltpu.VMEM((DENSE_TE, DENSE_TT), BF16),
        ],
        compiler_params=_cparams(("arbitrary",)),
        name="peer_dense",
    )(h2, down, upT, c2, e2, n1, e1, x1, gate, final_g)


def _transpose_kernel(x_ref, o_ref):
    o_ref[...] = x_ref[...].T.astype(BF16)


def _transpose_table(t):
    rows, cols = t.shape
    tr = 512
    return pl.pallas_call(
        _transpose_kernel,
        grid=(rows // tr,),
        in_specs=[pl.BlockSpec((tr, cols), lambda i: (i, 0))],
        out_specs=pl.BlockSpec((cols, tr), lambda i: (0, i)),
        out_shape=jax.ShapeDtypeStruct((cols, rows), BF16),
        compiler_params=_cparams(("arbitrary",)),
        name="table_transpose",
    )(t)


def _position_dft_tables():
    radix = 64
    r = lax.broadcasted_iota(jnp.int32, (radix, SEQ), 0)
    k = lax.broadcasted_iota(jnp.int32, (radix, SEQ), 1)
    coarse = ((r * k) % radix).astype(F32) * (2.0 * math.pi / radix)
    fine = ((r * k) % SEQ).astype(F32) * (2.0 * math.pi / SEQ)
    scale = 1.0 / math.sqrt(SEQ)
    ca, sa = jnp.cos(coarse)[:, None, :] * scale, jnp.sin(coarse)[:, None, :] * scale
    cb, sb = jnp.cos(fine)[None, :, :], jnp.sin(fine)[None, :, :]
    cos_tab = (ca * cb - sa * sb).reshape(SEQ, SEQ).astype(BF16)
    sin_tab = (sa * cb + ca * sb).reshape(SEQ, SEQ).astype(BF16)
    return cos_tab, sin_tab


def _channel_dft_table():
    j = lax.broadcasted_iota(jnp.int32, (FNET_GROUP, FNET_GROUP), 0)
    k = lax.broadcasted_iota(jnp.int32, (FNET_GROUP, FNET_GROUP), 1)
    ang = ((j * k) % FNET_GROUP).astype(F32) * (2.0 * math.pi / FNET_GROUP)
    scale = 1.0 / math.sqrt(FNET_GROUP)
    return jnp.concatenate([jnp.cos(ang) * scale, jnp.sin(ang) * scale], axis=1).astype(BF16)


def _pad_lanes(v, width):
    return jnp.pad(v, ((0, 0), (0, width - v.shape[1])))


def kernel(x, c, w_ada, b_ada, norm_mix_g, w_in, conv_w, conv_b, a_log_fwd, a_log_bwd, dt_bias_fwd, dt_bias_bwd, d_skip, ssm_norm_g, w_out, norm_ffn_g, w_query, sub_keys, expert_down, expert_up, final_norm_g):
    assert w_ada.shape[0] == 1, "single-layer problem: the final RMSNorm is fused into the PEER kernel"
    xt = x.reshape(TOKENS, D_MODEL)
    c_pad = jnp.pad(c, ((0, SUBLANES - BATCH), (0, 0)))
    cs_tab, ss_tab = _position_dft_tables()
    csc_tab = _channel_dft_table()
    for layer in range(1):
        mod = _adaln(c_pad, w_ada[layer], b_ada[layer][None, :])[:BATCH]
        shift_m, scale_m, gate_m, shift_f, scale_f, gate_f = [
            m.reshape(BATCH, 1, D_MODEL) for m in jnp.split(mod, 6, axis=-1)]

        w = w_in[layer]
        wf = w[:, :D_FNET].astype(BF16)
        wz = w[:, D_FNET:D_FNET + D_SSM].astype(BF16)
        wx = w[:, D_FNET + D_SSM:D_FNET + D_SSM + D_CONV].astype(BF16)
        wdt = _pad_lanes(w[:, D_FNET + D_SSM + D_CONV:], DT_PAD).astype(BF16)
        uc, us, z, xbc, dt, dtT = _inproj(xt, norm_mix_g[layer][None, :], shift_m, scale_m,
                                          wf, wz, wx, wdt, wdt.T, csc_tab)
        y_fnet = _dft(cs_tab, ss_tab, uc, us)

        xconv = _conv(xbc.reshape(BATCH, SEQ, D_CONV), conv_w[layer], conv_b[layer][None, :])
        alog_row = _pad_lanes(jnp.concatenate([a_log_fwd[layer], a_log_bwd[layer]])[None, :], DT_PAD)
        bias_row = _pad_lanes(jnp.concatenate([dt_bias_fwd[layer], dt_bias_bwd[layer]])[None, :], DT_PAD)
        dskip_row = jnp.repeat(d_skip[layer], SSM_HEAD_DIM)[None, :]
        y_ssm = _ssd(xconv.reshape(TOKENS, D_CONV), dt, dtT, z, alog_row, bias_row, alog_row.T, bias_row.T,
                     dskip_row, ssm_norm_g[layer][None, :])

        wo = w_out[layer].astype(BF16)
        x1, h2 = _outproj(xt, y_fnet, y_ssm, wo[:D_FNET], wo[D_FNET:], gate_m,
                          norm_ffn_g[layer][None, :], shift_f, scale_f)

        keys = sub_keys[layer].reshape(2 * PEER_HEADS, PEER_KEYS, PEER_HALF).astype(BF16)
        c2, e2, n1, e1 = _plan(h2, w_query[layer].astype(BF16), keys)
        xt = _dense(h2, expert_down[layer].astype(BF16), _transpose_table(expert_up[layer]),
                    c2, e2, n1, e1, x1, gate_f, final_norm_g[None, :])
    return xt.reshape(BATCH, SEQ, D_MODEL)
```

```python
import functools
import math

import jax
import jax.numpy as jnp
from jax import lax
from jax.experimental import pallas as pl
from jax.experimental.pallas import tpu as pltpu

F32 = jnp.float32
BF16 = jnp.bfloat16
HIGHEST = lax.Precision.HIGHEST

D_MODEL = 1024
BATCH = 4
SEQ = 4096
TOKENS = BATCH * SEQ
D_MIX = 2 * D_MODEL
D_FNET = D_MIX // 4
N_FNET_GROUPS = 4
FNET_GROUP = D_FNET // N_FNET_GROUPS
D_SSM = D_MIX - D_FNET
SSM_HEAD_DIM = 64
N_SSM_HEADS = D_SSM // SSM_HEAD_DIM
N_BC_GROUPS = 4
HEADS_PER_GROUP = N_SSM_HEADS // N_BC_GROUPS
D_STATE = 128
CONV_WIDTH = 5
SSD_CHUNK = 128
N_CHUNKS = SEQ // SSD_CHUNK
D_BC = N_BC_GROUPS * D_STATE
D_CONV = D_SSM + 2 * D_BC
GROUP_WIDTH = D_SSM // N_BC_GROUPS
PEER_HEADS = 8
PEER_KEYS = 128
PEER_EXPERTS = PEER_KEYS * PEER_KEYS
PEER_HALF = 128
PEER_TOPK = 16
EPS = 1e-6

LANES = 128
SUBLANES = 8
BF16_ROWS = 16
DT_PAD = LANES
VMEM_LIMIT = 56 * 1024 * 1024

ADA_TN = 512
PROJ_TM = 512
DFT_TM = 512
CONV_TC = 256
CONV_TR = 512
CONV_HALO = SUBLANES
OUT_TM = 512
PLAN_TT = 256
DENSE_TT = 512
DENSE_TE = 1024
DENSE_SUB = 256
DENSE_CHUNK = 256


def _cparams(sem):
    return pltpu.CompilerParams(dimension_semantics=sem, vmem_limit_bytes=VMEM_LIMIT)


def _sigmoid(x):
    return 1.0 / (1.0 + jnp.exp(-x))


def _softplus(x):
    return jnp.maximum(x, 0.0) + jnp.log(1.0 + jnp.exp(-jnp.abs(x)))


def _adaln_kernel(c_ref, w_ref, b_ref, o_ref):
    c = c_ref[...]
    ca = c * _sigmoid(c)
    o_ref[...] = jnp.dot(ca, w_ref[...], preferred_element_type=F32, precision=HIGHEST) + b_ref[...]


def _adaln(c_pad, w_ada, b_ada):
    n = w_ada.shape[1]
    return pl.pallas_call(
        _adaln_kernel,
        grid=(n // ADA_TN,),
        in_specs=[
            pl.BlockSpec((SUBLANES, D_MODEL), lambda j: (0, 0)),
            pl.BlockSpec((D_MODEL, ADA_TN), lambda j: (0, j)),
            pl.BlockSpec((1, ADA_TN), lambda j: (0, j)),
        ],
        out_specs=pl.BlockSpec((SUBLANES, ADA_TN), lambda j: (0, j)),
        out_shape=jax.ShapeDtypeStruct((SUBLANES, n), F32),
        compiler_params=_cparams(("arbitrary",)),
        name="adaln",
    )(c_pad, w_ada, b_ada)


def _modulated_norm(x, gain, shift, scale):
    ms = jnp.mean(x * x, axis=-1, keepdims=True)
    xn = x * lax.rsqrt(ms + EPS) * gain
    return xn * (1.0 + scale) + shift


def _inproj_kernel(x_ref, g_ref, shift_ref, scale_ref, wf_ref, wz_ref, wx_ref, wdt_ref, wdtT_ref, csc_ref,
                   uc_ref, us_ref, z_ref, xbc_ref, dt_ref, dtT_ref):
    h = _modulated_norm(x_ref[...], g_ref[...], shift_ref[0], scale_ref[0])
    hb = h.astype(BF16)
    fb = jnp.dot(hb, wf_ref[...], preferred_element_type=F32).astype(BF16)
    for g in range(N_FNET_GROUPS):
        sl = slice(g * FNET_GROUP, (g + 1) * FNET_GROUP)
        u = jnp.dot(fb[:, sl], csc_ref[...], preferred_element_type=F32)
        uc_ref[:, sl] = u[:, :FNET_GROUP].astype(BF16)
        us_ref[:, sl] = u[:, FNET_GROUP:].astype(BF16)
    z_ref[...] = jnp.dot(hb, wz_ref[...], preferred_element_type=F32).astype(BF16)
    xbc_ref[...] = jnp.dot(hb, wx_ref[...], preferred_element_type=F32).astype(BF16)
    dt_ref[...] = jnp.dot(hb, wdt_ref[...], preferred_element_type=F32)
    dtT_ref[...] = lax.dot_general(wdtT_ref[...], hb, (((1,), (1,)), ((), ())), preferred_element_type=F32)


def _inproj(x2d, gain, shift, scale, wf, wz, wx, wdt, wdtT, csc):
    tiles_per_batch = SEQ // PROJ_TM
    row = lambda i: (i, 0)
    const = lambda i: (0, 0)
    per_batch = lambda i: (i // tiles_per_batch, 0, 0)
    return pl.pallas_call(
        _inproj_kernel,
        grid=(TOKENS // PROJ_TM,),
        in_specs=[
            pl.BlockSpec((PROJ_TM, D_MODEL), row),
            pl.BlockSpec((1, D_MODEL), const),
            pl.BlockSpec((1, 1, D_MODEL), per_batch),
            pl.BlockSpec((1, 1, D_MODEL), per_batch),
            pl.BlockSpec(wf.shape, const),
            pl.BlockSpec(wz.shape, const),
            pl.BlockSpec(wx.shape, const),
            pl.BlockSpec(wdt.shape, const),
            pl.BlockSpec(wdtT.shape, const),
            pl.BlockSpec(csc.shape, const),
        ],
        out_specs=[
            pl.BlockSpec((PROJ_TM, D_FNET), row),
            pl.BlockSpec((PROJ_TM, D_FNET), row),
            pl.BlockSpec((PROJ_TM, D_SSM), row),
            pl.BlockSpec((PROJ_TM, D_CONV), row),
            pl.BlockSpec((PROJ_TM, DT_PAD), row),
            pl.BlockSpec((DT_PAD, PROJ_TM), lambda i: (0, i)),
        ],
        out_shape=[
            jax.ShapeDtypeStruct((TOKENS, D_FNET), BF16),
            jax.ShapeDtypeStruct((TOKENS, D_FNET), BF16),
            jax.ShapeDtypeStruct((TOKENS, D_SSM), BF16),
            jax.ShapeDtypeStruct((TOKENS, D_CONV), BF16),
            jax.ShapeDtypeStruct((TOKENS, DT_PAD), F32),
            jax.ShapeDtypeStruct((DT_PAD, TOKENS), F32),
        ],
        compiler_params=_cparams(("arbitrary",)),
        name="inproj",
    )(x2d, gain, shift, scale, wf, wz, wx, wdt, wdtT, csc)


def _dft_kernel(cs_ref, ss_ref, uc_ref, us_ref, o_ref):
    y = jnp.dot(cs_ref[...], uc_ref[...], preferred_element_type=F32)
    y = y - jnp.dot(ss_ref[...], us_ref[...], preferred_element_type=F32)
    o_ref[...] = y.astype(BF16)


def _dft(cs, ss, uc, us):
    mt = SEQ // DFT_TM
    return pl.pallas_call(
        _dft_kernel,
        grid=(BATCH, mt),
        in_specs=[
            pl.BlockSpec((DFT_TM, SEQ), lambda b, m: (m, 0)),
            pl.BlockSpec((DFT_TM, SEQ), lambda b, m: (m, 0)),
            pl.BlockSpec((SEQ, D_FNET), lambda b, m: (b, 0)),
            pl.BlockSpec((SEQ, D_FNET), lambda b, m: (b, 0)),
        ],
        out_specs=pl.BlockSpec((DFT_TM, D_FNET), lambda b, m: (b * mt + m, 0)),
        out_shape=jax.ShapeDtypeStruct((TOKENS, D_FNET), BF16),
        compiler_params=_cparams(("arbitrary", "arbitrary")),
        name="fnet_dft",
    )(cs, ss, uc, us)


def _conv_kernel(x_ref, w_ref, b_ref, o_ref, pad_ref):
    zeros = jnp.zeros((CONV_HALO, CONV_TC), F32)
    pad_ref[0:CONV_HALO, :] = zeros
    pad_ref[SEQ + CONV_HALO:SEQ + 2 * CONV_HALO, :] = zeros
    pad_ref[CONV_HALO:SEQ + CONV_HALO, :] = x_ref[0].astype(F32)
    w = w_ref[...]
    bias = b_ref[...]
    first = CONV_HALO - CONV_WIDTH // 2
    for i in range(SEQ // CONV_TR):
        acc = jnp.broadcast_to(bias, (CONV_TR, CONV_TC))
        for k in range(CONV_WIDTH):
            lo = first + k + i * CONV_TR
            acc = acc + pad_ref[lo:lo + CONV_TR, :] * w[k:k + 1, :]
        o_ref[0, i * CONV_TR:(i + 1) * CONV_TR, :] = (acc * _sigmoid(acc)).astype(BF16)


def _conv(xbc3, conv_w, conv_b):
    return pl.pallas_call(
        _conv_kernel,
        grid=(BATCH, D_CONV // CONV_TC),
        in_specs=[
            pl.BlockSpec((1, SEQ, CONV_TC), lambda b, j: (b, 0, j)),
            pl.BlockSpec((CONV_WIDTH, CONV_TC), lambda b, j: (0, j)),
            pl.BlockSpec((1, CONV_TC), lambda b, j: (0, j)),
        ],
        out_specs=pl.BlockSpec((1, SEQ, CONV_TC), lambda b, j: (b, 0, j)),
        out_shape=jax.ShapeDtypeStruct((BATCH, SEQ, D_CONV), BF16),
        scratch_shapes=[pltpu.VMEM((SEQ + 2 * CONV_HALO, CONV_TC), F32)],
        compiler_params=_cparams(("arbitrary", "arbitrary")),
        name="conv_silu",
    )(xbc3, conv_w, conv_b)


def _ssd_direction(xc, dtvT, cs, csT, a, aT, decay_row, state_ref, yacc_ref, head_base, forward):
    li = lax.broadcasted_iota(jnp.int32, (SSD_CHUNK, SSD_CHUNK), 0)
    si = lax.broadcasted_iota(jnp.int32, (SSD_CHUNK, SSD_CHUNK), 1)
    low_half = si < SSM_HEAD_DIM
    if forward:
        u, uT = cs, csT
        mask = li >= si
    else:
        u, uT = cs - a, csT - aT
        mask = si >= li
    pair = 2 * SSM_HEAD_DIM
    for g in range(N_BC_GROUPS):
        bc = xc[:, D_SSM + g * D_STATE:D_SSM + (g + 1) * D_STATE]
        cc = xc[:, D_SSM + D_BC + g * D_STATE:D_SSM + D_BC + (g + 1) * D_STATE]
        gmat = lax.dot_general(cc, bc, (((1,), (1,)), ((), ())), preferred_element_type=F32)
        bcT = bc.astype(F32).T
        ccf = cc.astype(F32)
        for pr in range(HEADS_PER_GROUP // 2):
            h0 = g * HEADS_PER_GROUP + 2 * pr
            xcols = slice(h0 * SSM_HEAD_DIM, h0 * SSM_HEAD_DIM + pair)
            scols = slice(pr * pair, (pr + 1) * pair)
            xpair = xc[:, xcols]
            spair = state_ref[g, :, scols]
            rhs = jnp.concatenate([xpair, spair.astype(BF16)], axis=0)
            ys, upds = [], []
            for j in range(2):
                hh = head_base + h0 + j
                ub = jnp.broadcast_to(u[:, hh:hh + 1], (SSD_CHUNK, SSD_CHUNK))
                urow = uT[hh:hh + 1, :]
                dtrow = dtvT[hh:hh + 1, :]
                tot = cs[SSD_CHUNK - 1:SSD_CHUNK, hh:hh + 1]
                if forward:
                    diff = ub - urow
                    cscale = jnp.exp(ub)
                    wrow = jnp.exp(tot - urow) * dtrow
                else:
                    diff = urow - ub
                    cscale = jnp.exp(tot - ub)
                    wrow = jnp.exp(urow) * dtrow
                lmat = jnp.where(mask, jnp.exp(jnp.minimum(diff, 0.0)), 0.0)
                lhs = jnp.concatenate([(gmat * lmat * dtrow).astype(BF16), (ccf * cscale).astype(BF16)], axis=1)
                ys.append(jnp.dot(lhs, rhs, preferred_element_type=F32))
                upds.append(jnp.dot((bcT * wrow).astype(BF16), xpair, preferred_element_type=F32))
            yacc_ref[:, xcols] = jnp.where(low_half, ys[0], ys[1])
            state_ref[g, :, scols] = spair * decay_row[:, xcols] + jnp.where(low_half, upds[0], upds[1])


def _ssd_kernel(xc_ref, dt_ref, dtT_ref, z_ref, alog_row_ref, bias_row_ref, alog_col_ref, bias_col_ref,
                dskip_ref, ng_ref, expand_ref, y_ref, state_ref, yacc_ref, yb_ref):
    phase = pl.program_id(1)
    step = pl.program_id(2)

    @pl.when(step == 0)
    def _():
        state_ref[...] = jnp.zeros(state_ref.shape, F32)

    xc = xc_ref[...]
    dtv = _softplus(dt_ref[...] + bias_row_ref[...])
    a = dtv * (-jnp.exp(alog_row_ref[...]))
    dtvT = _softplus(dtT_ref[...] + bias_col_ref[...])
    aT = dtvT * (-jnp.exp(alog_col_ref[...]))
    li = lax.broadcasted_iota(jnp.int32, (SSD_CHUNK, SSD_CHUNK), 0)
    si = lax.broadcasted_iota(jnp.int32, (SSD_CHUNK, SSD_CHUNK), 1)
    lower = (li >= si).astype(F32)
    upper = (li <= si).astype(F32)
    cs = jnp.dot(lower, a, preferred_element_type=F32, precision=HIGHEST)
    csT = jnp.dot(aT, upper, preferred_element_type=F32, precision=HIGHEST)
    chunk_decay = jnp.exp(cs[SSD_CHUNK - 1:SSD_CHUNK, :])

    def decay_row(direction):
        return jnp.dot(chunk_decay, expand_ref[direction], preferred_element_type=F32, precision=HIGHEST)

    @pl.when(phase == 0)
    def _():
        chunk = N_CHUNKS - 1 - step
        _ssd_direction(xc, dtvT, cs, csT, a, aT, decay_row(1), state_ref, yacc_ref, N_SSM_HEADS, False)
        yb_ref[chunk] = yacc_ref[...].astype(BF16)

    @pl.when(phase == 1)
    def _():
        _ssd_direction(xc, dtvT, cs, csT, a, aT, decay_row(0), state_ref, yacc_ref, 0, True)
        xs = xc[:, :D_SSM].astype(F32)
        y = yacc_ref[...] + yb_ref[step].astype(F32) + xs * dskip_ref[...]
        z = z_ref[...].astype(F32)
        y = y * (z * _sigmoid(z))
        for g in range(N_BC_GROUPS):
            gs = slice(g * GROUP_WIDTH, (g + 1) * GROUP_WIDTH)
            yg = y[:, gs]
            ms = jnp.mean(yg * yg, axis=-1, keepdims=True)
            y_ref[:, gs] = (yg * lax.rsqrt(ms + EPS) * ng_ref[:, gs]).astype(BF16)


def _head_expansion():
    col = lax.broadcasted_iota(jnp.int32, (2, DT_PAD, D_SSM), 1)
    head = lax.broadcasted_iota(jnp.int32, (2, DT_PAD, D_SSM), 2) // SSM_HEAD_DIM
    direction = lax.broadcasted_iota(jnp.int32, (2, DT_PAD, D_SSM), 0)
    return (col == direction * N_SSM_HEADS + head).astype(F32)


def _ssd(xconv, dt, dtT, z, alog_row, bias_row, alog_col, bias_col, dskip_row, norm_g):
    def chunk_of(p, s):
        return p * s + (1 - p) * (N_CHUNKS - 1 - s)

    rows = lambda b, p, s: (b * N_CHUNKS + chunk_of(p, s), 0)
    cols = lambda b, p, s: (0, b * N_CHUNKS + chunk_of(p, s))
    fwd_only = lambda b, p, s: (b * N_CHUNKS + p * s, 0)
    const = lambda b, p, s: (0, 0)
    return pl.pallas_call(
        _ssd_kernel,
        grid=(BATCH, 2, N_CHUNKS),
        in_specs=[
            pl.BlockSpec((SSD_CHUNK, D_CONV), rows),
            pl.BlockSpec((SSD_CHUNK, DT_PAD), rows),
            pl.BlockSpec((DT_PAD, SSD_CHUNK), cols),
            pl.BlockSpec((SSD_CHUNK, D_SSM), fwd_only),
            pl.BlockSpec((1, DT_PAD), const),
            pl.BlockSpec((1, DT_PAD), const),
            pl.BlockSpec((DT_PAD, 1), const),
            pl.BlockSpec((DT_PAD, 1), const),
            pl.BlockSpec((1, D_SSM), const),
            pl.BlockSpec((1, D_SSM), const),
            pl.BlockSpec((2, DT_PAD, D_SSM), lambda b, p, s: (0, 0, 0)),
        ],
        out_specs=pl.BlockSpec((SSD_CHUNK, D_SSM), fwd_only),
        out_shape=jax.ShapeDtypeStruct((TOKENS, D_SSM), BF16),
        scratch_shapes=[
            pltpu.VMEM((N_BC_GROUPS, D_STATE, GROUP_WIDTH), F32),
            pltpu.VMEM((SSD_CHUNK, D_SSM), F32),
            pltpu.VMEM((N_CHUNKS, SSD_CHUNK, D_SSM), BF16),
        ],
        compiler_params=_cparams(("arbitrary", "arbitrary", "arbitrary")),
        name="ssd_scan",
    )(xconv, dt, dtT, z, alog_row, bias_row, alog_col, bias_col, dskip_row, norm_g, _head_expansion())


def _outproj_kernel(x_ref, yf_ref, ys_ref, wf_ref, ws_ref, gate_ref, g_ref, shift_ref, scale_ref, x1_ref, h2_ref):
    mix = jnp.dot(yf_ref[...], wf_ref[...], preferred_element_type=F32)
    mix = mix + jnp.dot(ys_ref[...], ws_ref[...], preferred_element_type=F32)
    x1 = x_ref[...] + gate_ref[0] * mix
    x1_ref[...] = x1
    h2_ref[...] = _modulated_norm(x1, g_ref[...], shift_ref[0], scale_ref[0]).astype(BF16)


def _outproj(x2d, yf, ys, wf, ws, gate, gain, shift, scale):
    tiles_per_batch = SEQ // OUT_TM
    row = lambda i: (i, 0)
    const = lambda i: (0, 0)
    per_batch = lambda i: (i // tiles_per_batch, 0, 0)
    return pl.pallas_call(
        _outproj_kernel,
        grid=(TOKENS // OUT_TM,),
        in_specs=[
            pl.BlockSpec((OUT_TM, D_MODEL), row),
            pl.BlockSpec((OUT_TM, D_FNET), row),
            pl.BlockSpec((OUT_TM, D_SSM), row),
            pl.BlockSpec(wf.shape, const),
            pl.BlockSpec(ws.shape, const),
            pl.BlockSpec((1, 1, D_MODEL), per_batch),
            pl.BlockSpec((1, D_MODEL), const),
            pl.BlockSpec((1, 1, D_MODEL), per_batch),
            pl.BlockSpec((1, 1, D_MODEL), per_batch),
        ],
        out_specs=[pl.BlockSpec((OUT_TM, D_MODEL), row), pl.BlockSpec((OUT_TM, D_MODEL), row)],
        out_shape=[jax.ShapeDtypeStruct((TOKENS, D_MODEL), F32), jax.ShapeDtypeStruct((TOKENS, D_MODEL), BF16)],
        compiler_params=_cparams(("arbitrary",)),
        name="outproj",
    )(x2d, yf, ys, wf, ws, gate, gain, shift, scale)


def _extract_topk(x, k, break_ties):
    rows = x.shape[0]
    iota = lax.broadcasted_iota(jnp.int32, x.shape, 0).astype(F32)
    rank = jnp.full(x.shape, float(k), F32)
    work = x
    vals = []
    for r in range(k):
        m = jnp.max(work, axis=0, keepdims=True)
        sel = work == m
        if break_ties:
            first = jnp.min(jnp.where(sel, iota, float(rows)), axis=0, keepdims=True)
            sel = iota == first
        rank = jnp.where(sel, float(r), rank)
        work = jnp.where(sel, -jnp.inf, work)
        vals.append(m)
    return jnp.concatenate(vals, axis=0), rank


def _candidate_width(r):
    return PEER_TOPK // (r + 1)


def _candidates(v1, v2):
    sub = lax.broadcasted_iota(jnp.int32, (SUBLANES, v1.shape[1]), 0)
    blocks = [v1[0:1] + v2, v1[1:2] + v2[0:SUBLANES]]
    for r in range(2, SUBLANES):
        blocks.append(jnp.where(sub < _candidate_width(r), v1[r:r + 1] + v2[0:SUBLANES], -jnp.inf))
    blocks.append(v1[SUBLANES:] + v2[0:1])
    return jnp.concatenate(blocks, axis=0)


def _row_counts(chosen):
    counts = [jnp.sum(chosen[0:PEER_TOPK], axis=0, keepdims=True)]
    for r in range(1, SUBLANES):
        lo = PEER_TOPK + (r - 1) * SUBLANES
        counts.append(jnp.sum(chosen[lo:lo + SUBLANES], axis=0, keepdims=True))
    tail = chosen[PEER_TOPK + (SUBLANES - 1) * SUBLANES:]
    counts.extend(tail[j:j + 1] for j in range(SUBLANES))
    return counts


def _plan_head(s1, s2, break_ties):
    v1, r1 = _extract_topk(s1, PEER_TOPK, break_ties)
    v2, r2 = _extract_topk(s2, PEER_TOPK, break_ties)
    cand = _candidates(v1, v2)
    _, crank = _extract_topk(cand, PEER_TOPK, break_ties)
    chosen = (crank < float(PEER_TOPK)).astype(F32)
    counts = _row_counts(chosen)
    top = v1[0:1] + v2[0:1]
    zsum = jnp.sum(chosen * jnp.exp(cand - top), axis=0, keepdims=True)
    r1b = r1.astype(BF16)
    n1 = jnp.zeros(s1.shape, BF16)
    for r in range(PEER_TOPK):
        n1 = jnp.where(r1b == float(r), counts[r].astype(BF16), n1)
    e1 = jnp.exp(s1 - v1[0:1]) / zsum
    e2 = jnp.exp(s2 - v2[0:1])
    ranked = (r1 < float(PEER_TOPK)).astype(F32) + (r2 < float(PEER_TOPK)).astype(F32)
    total = jnp.sum(ranked, axis=0, keepdims=True) + jnp.sum(chosen, axis=0, keepdims=True)
    clean = jnp.min(jnp.where(total == 3.0 * PEER_TOPK, 1.0, 0.0)) > 0.5
    return r2, e2, n1.astype(F32), e1, clean


def _twin_bf16_words(x):
    hi = pltpu.bitcast(x.astype(BF16).astype(F32), jnp.uint32)
    return hi | (hi >> 16)


def _plan_kernel(h2_ref, wq_ref, keys_ref, c2_ref, e2_ref, n1_ref, e1_ref, q_ref):
    q_ref[...] = jnp.dot(h2_ref[...], wq_ref[...], preferred_element_type=F32).astype(BF16)

    def per_head(h, carry):
        def scores(side):
            col = pl.multiple_of((2 * h + side) * PEER_HALF, PEER_HALF)
            qh = q_ref[:, pl.ds(col, PEER_HALF)]
            return lax.dot_general(keys_ref[2 * h + side], qh, (((1,), (1,)), ((), ())), preferred_element_type=F32)

        s1 = scores(0)
        s2 = scores(1)

        def store(r2, e2, n1, e1):
            c2_ref[h] = r2.astype(BF16)
            e2_ref[h] = e2.astype(BF16)
            for c in range(PLAN_TT // LANES):
                cols = slice(c * LANES, (c + 1) * LANES)
                n1_ref[h, c] = _twin_bf16_words(n1[:, cols])
                e1_ref[h, c] = _twin_bf16_words(e1[:, cols])

        r2, e2, n1, e1, clean = _plan_head(s1, s2, break_ties=False)
        store(r2, e2, n1, e1)

        @pl.when(jnp.logical_not(clean))
        def _():
            store(*_plan_head(s1, s2, break_ties=True)[:4])

        return carry

    lax.fori_loop(0, PEER_HEADS, per_head, 0)


def _plan(h2, wq, keys):
    blk = lambda i: (0, 0, i)
    shp = (PEER_HEADS, PEER_KEYS, TOKENS)
    lane_blk = lambda i: (0, i, 0, 0)
    lane_shp = (PEER_HEADS, TOKENS // LANES, PEER_KEYS, LANES)
    lane_spec = pl.BlockSpec((PEER_HEADS, PLAN_TT // LANES, PEER_KEYS, LANES), lane_blk)
    return pl.pallas_call(
        _plan_kernel,
        grid=(TOKENS // PLAN_TT,),
        in_specs=[
            pl.BlockSpec((PLAN_TT, D_MODEL), lambda i: (i, 0)),
            pl.BlockSpec(wq.shape, lambda i: (0, 0)),
            pl.BlockSpec(keys.shape, lambda i: (0, 0, 0)),
        ],
        out_specs=[pl.BlockSpec((PEER_HEADS, PEER_KEYS, PLAN_TT), blk)] * 2 + [lane_spec] * 2,
        out_shape=[
            jax.ShapeDtypeStruct(shp, BF16),
            jax.ShapeDtypeStruct(shp, BF16),
            jax.ShapeDtypeStruct(lane_shp, jnp.uint32),
            jax.ShapeDtypeStruct(lane_shp, jnp.uint32),
        ],
        scratch_shapes=[pltpu.VMEM((PLAN_TT, 2 * PEER_HEADS * PEER_HALF), BF16)],
        compiler_params=_cparams(("arbitrary",)),
        name="peer_plan",
    )(h2, wq, keys)


def _gelu(x):
    return 0.5 * x * (1.0 + lax.erf(x * (1.0 / math.sqrt(2.0))))


DENSE_NE = PEER_EXPERTS // DENSE_TE
DENSE_NT = TOKENS // DENSE_TT
DENSE_TILES = DENSE_NT * DENSE_NE
DENSE_KEYS_PER_TILE = DENSE_TE // PEER_KEYS


def _dense_stage_body(f, h2_ref, down_ref, upT_ref, c2_ref, e2_ref, n1_ref, e1_ref, acc_ref,
                      act_w, act_r, w_w, w_r):
    gate_tile = jnp.clip(f - 1, 0, DENSE_TILES - 1) % DENSE_NE
    gate_valid = jnp.logical_and(f >= 1, f <= DENSE_TILES)

    def act_piece(q):
        rows = slice(q * DENSE_SUB, (q + 1) * DENSE_SUB)
        act_w[rows, :] = lax.dot_general(down_ref[rows, :], h2_ref[...], (((1,), (1,)), ((), ())),
                                         preferred_element_type=F32)

    def gate_piece(k):
        i1 = gate_tile * DENSE_KEYS_PER_TILE + k
        for c in range(DENSE_TT // DENSE_CHUNK):
            cols = slice(c * DENSE_CHUNK, (c + 1) * DENSE_CHUNK)
            gates = jnp.zeros((PEER_KEYS, DENSE_CHUNK), BF16)
            for h in range(PEER_HEADS):
                def row_tile(ref):
                    tiles = []
                    for t in range(DENSE_CHUNK // LANES):
                        lane_tile = c * (DENSE_CHUNK // LANES) + t
                        words = jnp.broadcast_to(ref[h, lane_tile, pl.ds(i1, 1), :], (SUBLANES, LANES))
                        row = pltpu.bitcast(words, BF16)
                        tiles.append(jnp.concatenate([row] * (PEER_KEYS // BF16_ROWS), axis=0))
                    return jnp.concatenate(tiles, axis=1)

                hit = c2_ref[h, :, cols] < row_tile(n1_ref)
                gates = gates + jnp.where(hit, e2_ref[h, :, cols], jnp.zeros((), BF16)) * row_tile(e1_ref)
            rows = slice(k * PEER_KEYS, (k + 1) * PEER_KEYS)
            w = gates * _gelu(act_r[rows, cols]).astype(BF16)
            w_w[rows, cols] = jnp.where(gate_valid, w, jnp.zeros((), BF16))

    def out_piece(q):
        rows = slice(q * DENSE_SUB, (q + 1) * DENSE_SUB)
        acc_ref[rows, :] += jnp.dot(upT_ref[rows, :], w_r[...], preferred_element_type=F32)

    gates_per_sub = DENSE_SUB // PEER_KEYS
    for q in range(DENSE_TE // DENSE_SUB):
        act_piece(q)
        gate_piece(gates_per_sub * q)
        out_piece(q)
        for k in range(1, gates_per_sub):
            gate_piece(gates_per_sub * q + k)


def _dense_kernel(h2_ref, down_ref, upT_ref, c2_ref, e2_ref, n1_ref, e1_ref, x1_ref, gate_ref, g_ref,
                  o_ref, acc_ref, act0_ref, act1_ref, w0_ref, w1_ref):
    f = pl.program_id(0)
    out_tile = jnp.clip(f - 2, 0, DENSE_TILES - 1) % DENSE_NE

    @pl.when(f == 0)
    def _():
        act1_ref[...] = jnp.zeros(act1_ref.shape, F32)
        w0_ref[...] = jnp.zeros(w0_ref.shape, BF16)

    @pl.when(out_tile == 0)
    def _():
        acc_ref[...] = jnp.zeros(acc_ref.shape, F32)

    stage = functools.partial(_dense_stage_body, f, h2_ref, down_ref, upT_ref, c2_ref, e2_ref, n1_ref, e1_ref,
                              acc_ref)

    @pl.when(f % 2 == 0)
    def _():
        stage(act0_ref, act1_ref, w1_ref, w0_ref)

    @pl.when(f % 2 == 1)
    def _():
        stage(act1_ref, act0_ref, w0_ref, w1_ref)

    @pl.when(jnp.logical_and(out_tile == DENSE_NE - 1, f >= 2))
    def _():
        x2 = x1_ref[...] + gate_ref[0] * acc_ref[...].T
        ms = jnp.mean(x2 * x2, axis=-1, keepdims=True)
        o_ref[...] = x2 * lax.rsqrt(ms + EPS) * g_ref[...]


def _dense(h2, down, upT, c2, e2, n1, e1, x1, gate, final_g):
    tiles_per_batch = SEQ // DENSE_TT

    def tile(f, lag):
        return jnp.clip(f - lag, 0, DENSE_TILES - 1)

    plan = lambda f: (0, tile(f, 1) // DENSE_NE, 0, 0)
    out_tok = lambda f: (tile(f, 2) // DENSE_NE, 0)
    plan_blk = (PEER_HEADS, DENSE_TT // LANES, PEER_KEYS, LANES)
    rank_map = lambda f: (0, 0, tile(f, 1) // DENSE_NE)
    rank_blk = (PEER_HEADS, PEER_KEYS, DENSE_TT)
    return pl.pallas_call(
        _dense_kernel,
        grid=(DENSE_TILES + 2,),
        in_specs=[
            pl.BlockSpec((DENSE_TT, D_MODEL), lambda f: (tile(f, 0) // DENSE_NE, 0)),
            pl.BlockSpec((DENSE_TE, D_MODEL), lambda f: (tile(f, 0) % DENSE_NE, 0)),
            pl.BlockSpec((D_MODEL, DENSE_TE), lambda f: (0, tile(f, 2) % DENSE_NE)),
            pl.BlockSpec(rank_blk, rank_map),
            pl.BlockSpec(rank_blk, rank_map),
            pl.BlockSpec(plan_blk, plan),
            pl.BlockSpec(plan_blk, plan),
            pl.BlockSpec((DENSE_TT, D_MODEL), out_tok),
            pl.BlockSpec((1, 1, D_MODEL), lambda f: (tile(f, 2) // DENSE_NE // tiles_per_batch, 0, 0)),
            pl.BlockSpec((1, D_MODEL), lambda f: (0, 0)),
        ],
        out_specs=pl.BlockSpec((DENSE_TT, D_MODEL), out_tok),
        out_shape=jax.ShapeDtypeStruct((TOKENS, D_MODEL), F32),
        scratch_shapes=[
            pltpu.VMEM((D_MODEL, DENSE_TT), F32),
            pltpu.VMEM((DENSE_TE, DENSE_TT), F32),
            pltpu.VMEM((DENSE_TE, DENSE_TT), F32),
            pltpu.VMEM((DENSE_TE, DENSE_TT), BF16),
            pltpu.VMEM((DENSE_TE, DENSE_TT), BF16),
        ],
        compiler_params=_cparams(("arbitrary",)),
        name="peer_dense",
    )(h2, down, upT, c2, e2, n1, e1, x1, gate, final_g)


def _transpose_kernel(x_ref, o_ref):
    o_ref[...] = x_ref[...].T.astype(BF16)


def _transpose_table(t):
    rows, cols = t.shape
    tr = 512
    return pl.pallas_call(
        _transpose_kernel,
        grid=(rows // tr,),
        in_specs=[pl.BlockSpec((tr, cols), lambda i: (i, 0))],
        out_specs=pl.BlockSpec((cols, tr), lambda i: (0, i)),
        out_shape=jax.ShapeDtypeStruct((cols, rows), BF16),
        compiler_params=_cparams(("arbitrary",)),
        name="table_transpose",
    )(t)


def _position_dft_tables():
    radix = 64
    r = lax.broadcasted_iota(jnp.int32, (radix, SEQ), 0)
    k = lax.broadcasted_iota(jnp.int32, (radix, SEQ), 1)
    coarse = ((r * k) % radix).astype(F32) * (2.0 * math.pi / radix)
    fine = ((r * k) % SEQ).astype(F32) * (2.0 * math.pi / SEQ)
    scale = 1.0 / math.sqrt(SEQ)
    ca, sa = jnp.cos(coarse)[:, None, :] * scale, jnp.sin(coarse)[:, None, :] * scale
    cb, sb = jnp.cos(fine)[None, :, :], jnp.sin(fine)[None, :, :]
    cos_tab = (ca * cb - sa * sb).reshape(SEQ, SEQ).astype(BF16)
    sin_tab = (sa * cb + ca * sb).reshape(SEQ, SEQ).astype(BF16)
    return cos_tab, sin_tab


def _channel_dft_table():
    j = lax.broadcasted_iota(jnp.int32, (FNET_GROUP, FNET_GROUP), 0)
    k = lax.broadcasted_iota(jnp.int32, (FNET_GROUP, FNET_GROUP), 1)
    ang = ((j * k) % FNET_GROUP).astype(F32) * (2.0 * math.pi / FNET_GROUP)
    scale = 1.0 / math.sqrt(FNET_GROUP)
    return jnp.concatenate([jnp.cos(ang) * scale, jnp.sin(ang) * scale], axis=1).astype(BF16)


def _pad_lanes(v, width):
    return jnp.pad(v, ((0, 0), (0, width - v.shape[1])))


def kernel(x, c, w_ada, b_ada, norm_mix_g, w_in, conv_w, conv_b, a_log_fwd, a_log_bwd, dt_bias_fwd, dt_bias_bwd, d_skip, ssm_norm_g, w_out, norm_ffn_g, w_query, sub_keys, expert_down, expert_up, final_norm_g):
    assert w_ada.shape[0] == 1, "single-layer problem: the final RMSNorm is fused into the PEER kernel"
    xt = x.reshape(TOKENS, D_MODEL)
    c_pad = jnp.pad(c, ((0, SUBLANES - BATCH), (0, 0)))
    cs_tab, ss_tab = _position_dft_tables()
    csc_tab = _channel_dft_table()
    for layer in range(1):
        mod = _adaln(c_pad, w_ada[layer], b_ada[layer][None, :])[:BATCH]
        shift_m, scale_m, gate_m, shift_f, scale_f, gate_f = [
            m.reshape(BATCH, 1, D_MODEL) for m in jnp.split(mod, 6, axis=-1)]

        w = w_in[layer]
        wf = w[:, :D_FNET].astype(BF16)
        wz = w[:, D_FNET:D_FNET + D_SSM].astype(BF16)
        wx = w[:, D_FNET + D_SSM:D_FNET + D_SSM + D_CONV].astype(BF16)
        wdt = _pad_lanes(w[:, D_FNET + D_SSM + D_CONV:], DT_PAD).astype(BF16)
        uc, us, z, xbc, dt, dtT = _inproj(xt, norm_mix_g[layer][None, :], shift_m, scale_m,
                                          wf, wz, wx, wdt, wdt.T, csc_tab)
        y_fnet = _dft(cs_tab, ss_tab, uc, us)

        xconv = _conv(xbc.reshape(BATCH, SEQ, D_CONV), conv_w[layer], conv_b[layer][None, :])
        alog_row = _pad_lanes(jnp.concatenate([a_log_fwd[layer], a_log_bwd[layer]])[None, :], DT_PAD)
        bias_row = _pad_lanes(jnp.concatenate([dt_bias_fwd[layer], dt_bias_bwd[layer]])[None, :], DT_PAD)
        dskip_row = jnp.repeat(d_skip[layer], SSM_HEAD_DIM)[None, :]
        y_ssm = _ssd(xconv.reshape(TOKENS, D_CONV), dt, dtT, z, alog_row, bias_row, alog_row.T, bias_row.T,
                     dskip_row, ssm_norm_g[layer][None, :])

        wo = w_out[layer].astype(BF16)
        x1, h2 = _outproj(xt, y_fnet, y_ssm, wo[:D_FNET], wo[D_FNET:], gate_m,
                          norm_ffn_g[layer][None, :], shift_f, scale_f)

        keys = sub_keys[layer].reshape(2 * PEER_HEADS, PEER_KEYS, PEER_HALF).astype(BF16)
        c2, e2, n1, e1 = _plan(h2, w_query[layer].astype(BF16), keys)
        xt = _dense(h2, expert_down[layer].astype(BF16), _transpose_table(expert_up[layer]),
                    c2, e2, n1, e1, x1, gate_f, final_norm_g[None, :])
    return xt.reshape(BATCH, SEQ, D_MODEL)
```

```python
import functools
import math

import jax
import jax.numpy as jnp
from jax import lax
from jax.experimental import pallas as pl
from jax.experimental.pallas import tpu as pltpu

F32 = jnp.float32
BF16 = jnp.bfloat16
HIGHEST = lax.Precision.HIGHEST

D_MODEL = 1024
BATCH = 4
SEQ = 4096
TOKENS = BATCH * SEQ
D_MIX = 2 * D_MODEL
D_FNET = D_MIX // 4
N_FNET_GROUPS = 4
FNET_GROUP = D_FNET // N_FNET_GROUPS
D_SSM = D_MIX - D_FNET
SSM_HEAD_DIM = 64
N_SSM_HEADS = D_SSM // SSM_HEAD_DIM
N_BC_GROUPS = 4
HEADS_PER_GROUP = N_SSM_HEADS // N_BC_GROUPS
D_STATE = 128
CONV_WIDTH = 5
SSD_CHUNK = 128
N_CHUNKS = SEQ // SSD_CHUNK
D_BC = N_BC_GROUPS * D_STATE
D_CONV = D_SSM + 2 * D_BC
GROUP_WIDTH = D_SSM // N_BC_GROUPS
PEER_HEADS = 8
PEER_KEYS = 128
PEER_EXPERTS = PEER_KEYS * PEER_KEYS
PEER_HALF = 128
PEER_TOPK = 16
EPS = 1e-6

LANES = 128
SUBLANES = 8
BF16_ROWS = 16
DT_PAD = LANES
VMEM_LIMIT = 56 * 1024 * 1024

ADA_TN = 512
PROJ_TM = 512
DFT_TM = 512
CONV_TC = 256
CONV_TR = 512
CONV_HALO = SUBLANES
OUT_TM = 512
PLAN_TT = 256
DENSE_TT = 512
DENSE_TE = 1024
DENSE_SUB = 256
DENSE_CHUNK = 256


def _cparams(sem):
    return pltpu.CompilerParams(dimension_semantics=sem, vmem_limit_bytes=VMEM_LIMIT)


def _sigmoid(x):
    return 1.0 / (1.0 + jnp.exp(-x))


def _bf16_terms(x):
    terms = []
    rest = x
    for _ in range(3):
        term = rest.astype(BF16)
        terms.append(term)
        rest = rest - term.astype(F32)
    return terms


def _softplus(x):
    return jnp.maximum(x, 0.0) + jnp.log(1.0 + jnp.exp(-jnp.abs(x)))


def _adaln_kernel(c_ref, w_ref, b_ref, o_ref):
    c = c_ref[...]
    ca = c * _sigmoid(c)
    o_ref[...] = jnp.dot(ca, w_ref[...], preferred_element_type=F32, precision=HIGHEST) + b_ref[...]


def _adaln(c_pad, w_ada, b_ada):
    n = w_ada.shape[1]
    return pl.pallas_call(
        _adaln_kernel,
        grid=(n // ADA_TN,),
        in_specs=[
            pl.BlockSpec((SUBLANES, D_MODEL), lambda j: (0, 0)),
            pl.BlockSpec((D_MODEL, ADA_TN), lambda j: (0, j)),
            pl.BlockSpec((1, ADA_TN), lambda j: (0, j)),
        ],
        out_specs=pl.BlockSpec((SUBLANES, ADA_TN), lambda j: (0, j)),
        out_shape=jax.ShapeDtypeStruct((SUBLANES, n), F32),
        compiler_params=_cparams(("arbitrary",)),
        name="adaln",
    )(c_pad, w_ada, b_ada)


def _modulated_norm(x, gain, shift, scale):
    ms = jnp.mean(x * x, axis=-1, keepdims=True)
    xn = x * lax.rsqrt(ms + EPS) * gain
    return xn * (1.0 + scale) + shift


def _inproj_kernel(x_ref, g_ref, shift_ref, scale_ref, wf_ref, wz_ref, wx_ref, wdt_ref, wdtT_ref, csc_ref,
                   uc_ref, us_ref, z_ref, xbc_ref, dt_ref, dtT_ref):
    h = _modulated_norm(x_ref[...], g_ref[...], shift_ref[0], scale_ref[0])
    hb = h.astype(BF16)
    fb = jnp.dot(hb, wf_ref[...], preferred_element_type=F32).astype(BF16)
    for g in range(N_FNET_GROUPS):
        sl = slice(g * FNET_GROUP, (g + 1) * FNET_GROUP)
        u = jnp.dot(fb[:, sl], csc_ref[...], preferred_element_type=F32)
        uc_ref[:, sl] = u[:, :FNET_GROUP].astype(BF16)
        us_ref[:, sl] = u[:, FNET_GROUP:].astype(BF16)
    z_ref[...] = jnp.dot(hb, wz_ref[...], preferred_element_type=F32).astype(BF16)
    xbc_ref[...] = jnp.dot(hb, wx_ref[...], preferred_element_type=F32).astype(BF16)
    dt_ref[...] = jnp.dot(hb, wdt_ref[...], preferred_element_type=F32)
    dtT_ref[...] = lax.dot_general(wdtT_ref[...], hb, (((1,), (1,)), ((), ())), preferred_element_type=F32)


def _inproj(x2d, gain, shift, scale, wf, wz, wx, wdt, wdtT, csc):
    tiles_per_batch = SEQ // PROJ_TM
    row = lambda i: (i, 0)
    const = lambda i: (0, 0)
    per_batch = lambda i: (i // tiles_per_batch, 0, 0)
    return pl.pallas_call(
        _inproj_kernel,
        grid=(TOKENS // PROJ_TM,),
        in_specs=[
            pl.BlockSpec((PROJ_TM, D_MODEL), row),
            pl.BlockSpec((1, D_MODEL), const),
            pl.BlockSpec((1, 1, D_MODEL), per_batch),
            pl.BlockSpec((1, 1, D_MODEL), per_batch),
            pl.BlockSpec(wf.shape, const),
            pl.BlockSpec(wz.shape, const),
            pl.BlockSpec(wx.shape, const),
            pl.BlockSpec(wdt.shape, const),
            pl.BlockSpec(wdtT.shape, const),
            pl.BlockSpec(csc.shape, const),
        ],
        out_specs=[
            pl.BlockSpec((PROJ_TM, D_FNET), row),
            pl.BlockSpec((PROJ_TM, D_FNET), row),
            pl.BlockSpec((PROJ_TM, D_SSM), row),
            pl.BlockSpec((PROJ_TM, D_CONV), row),
            pl.BlockSpec((PROJ_TM, DT_PAD), row),
            pl.BlockSpec((DT_PAD, PROJ_TM), lambda i: (0, i)),
        ],
        out_shape=[
            jax.ShapeDtypeStruct((TOKENS, D_FNET), BF16),
            jax.ShapeDtypeStruct((TOKENS, D_FNET), BF16),
            jax.ShapeDtypeStruct((TOKENS, D_SSM), BF16),
            jax.ShapeDtypeStruct((TOKENS, D_CONV), BF16),
            jax.ShapeDtypeStruct((TOKENS, DT_PAD), F32),
            jax.ShapeDtypeStruct((DT_PAD, TOKENS), F32),
        ],
        compiler_params=_cparams(("arbitrary",)),
        name="inproj",
    )(x2d, gain, shift, scale, wf, wz, wx, wdt, wdtT, csc)


def _dft_kernel(cs_ref, ss_ref, uc_ref, us_ref, o_ref):
    y = jnp.dot(cs_ref[...], uc_ref[...], preferred_element_type=F32)
    y = y - jnp.dot(ss_ref[...], us_ref[...], preferred_element_type=F32)
    o_ref[...] = y.astype(BF16)


def _dft(cs, ss, uc, us):
    mt = SEQ // DFT_TM
    return pl.pallas_call(
        _dft_kernel,
        grid=(BATCH, mt),
        in_specs=[
            pl.BlockSpec((DFT_TM, SEQ), lambda b, m: (m, 0)),
            pl.BlockSpec((DFT_TM, SEQ), lambda b, m: (m, 0)),
            pl.BlockSpec((SEQ, D_FNET), lambda b, m: (b, 0)),
            pl.BlockSpec((SEQ, D_FNET), lambda b, m: (b, 0)),
        ],
        out_specs=pl.BlockSpec((DFT_TM, D_FNET), lambda b, m: (b * mt + m, 0)),
        out_shape=jax.ShapeDtypeStruct((TOKENS, D_FNET), BF16),
        compiler_params=_cparams(("arbitrary", "arbitrary")),
        name="fnet_dft",
    )(cs, ss, uc, us)


def _conv_kernel(x_ref, w_ref, b_ref, o_ref, pad_ref):
    zeros = jnp.zeros((CONV_HALO, CONV_TC), F32)
    pad_ref[0:CONV_HALO, :] = zeros
    pad_ref[SEQ + CONV_HALO:SEQ + 2 * CONV_HALO, :] = zeros
    pad_ref[CONV_HALO:SEQ + CONV_HALO, :] = x_ref[0].astype(F32)
    w = w_ref[...]
    bias = b_ref[...]
    first = CONV_HALO - CONV_WIDTH // 2
    for i in range(SEQ // CONV_TR):
        acc = jnp.broadcast_to(bias, (CONV_TR, CONV_TC))
        for k in range(CONV_WIDTH):
            lo = first + k + i * CONV_TR
            acc = acc + pad_ref[lo:lo + CONV_TR, :] * w[k:k + 1, :]
        o_ref[0, i * CONV_TR:(i + 1) * CONV_TR, :] = (acc * _sigmoid(acc)).astype(BF16)


def _conv(xbc3, conv_w, conv_b):
    return pl.pallas_call(
        _conv_kernel,
        grid=(BATCH, D_CONV // CONV_TC),
        in_specs=[
            pl.BlockSpec((1, SEQ, CONV_TC), lambda b, j: (b, 0, j)),
            pl.BlockSpec((CONV_WIDTH, CONV_TC), lambda b, j: (0, j)),
            pl.BlockSpec((1, CONV_TC), lambda b, j: (0, j)),
        ],
        out_specs=pl.BlockSpec((1, SEQ, CONV_TC), lambda b, j: (b, 0, j)),
        out_shape=jax.ShapeDtypeStruct((BATCH, SEQ, D_CONV), BF16),
        scratch_shapes=[pltpu.VMEM((SEQ + 2 * CONV_HALO, CONV_TC), F32)],
        compiler_params=_cparams(("arbitrary", "arbitrary")),
        name="conv_silu",
    )(xbc3, conv_w, conv_b)


def _ssd_direction(xc, dtvT, cs, csT, a, aT, decay_row, state_ref, yacc_ref, head_base, forward):
    li = lax.broadcasted_iota(jnp.int32, (SSD_CHUNK, SSD_CHUNK), 0)
    si = lax.broadcasted_iota(jnp.int32, (SSD_CHUNK, SSD_CHUNK), 1)
    low_half = si < SSM_HEAD_DIM
    if forward:
        u, uT = cs, csT
        mask = li >= si
    else:
        u, uT = cs - a, csT - aT
        mask = si >= li
    pair = 2 * SSM_HEAD_DIM
    for g in range(N_BC_GROUPS):
        bc = xc[:, D_SSM + g * D_STATE:D_SSM + (g + 1) * D_STATE]
        cc = xc[:, D_SSM + D_BC + g * D_STATE:D_SSM + D_BC + (g + 1) * D_STATE]
        gmat = lax.dot_general(cc, bc, (((1,), (1,)), ((), ())), preferred_element_type=F32)
        bcT = bc.astype(F32).T
        ccf = cc.astype(F32)
        for pr in range(HEADS_PER_GROUP // 2):
            h0 = g * HEADS_PER_GROUP + 2 * pr
            xcols = slice(h0 * SSM_HEAD_DIM, h0 * SSM_HEAD_DIM + pair)
            scols = slice(pr * pair, (pr + 1) * pair)
            xpair = xc[:, xcols]
            spair = state_ref[g, :, scols]
            rhs = jnp.concatenate([xpair, spair.astype(BF16)], axis=0)
            ys, upds = [], []
            for j in range(2):
                hh = head_base + h0 + j
                ub = jnp.broadcast_to(u[:, hh:hh + 1], (SSD_CHUNK, SSD_CHUNK))
                urow = uT[hh:hh + 1, :]
                dtrow = dtvT[hh:hh + 1, :]
                tot = cs[SSD_CHUNK - 1:SSD_CHUNK, hh:hh + 1]
                if forward:
                    diff = ub - urow
                    cscale = jnp.exp(ub)
                    wrow = jnp.exp(tot - urow) * dtrow
                else:
                    diff = urow - ub
                    cscale = jnp.exp(tot - ub)
                    wrow = jnp.exp(urow) * dtrow
                lmat = jnp.where(mask, jnp.exp(jnp.minimum(diff, 0.0)), 0.0)
                lhs = jnp.concatenate([(gmat * lmat * dtrow).astype(BF16), (ccf * cscale).astype(BF16)], axis=1)
                ys.append(jnp.dot(lhs, rhs, preferred_element_type=F32))
                upds.append(jnp.dot((bcT * wrow).astype(BF16), xpair, preferred_element_type=F32))
            yacc_ref[:, xcols] = jnp.where(low_half, ys[0], ys[1])
            state_ref[g, :, scols] = spair * decay_row[:, xcols] + jnp.where(low_half, upds[0], upds[1])


def _ssd_kernel(xc_ref, dt_ref, dtT_ref, z_ref, alog_row_ref, bias_row_ref, alog_col_ref, bias_col_ref,
                dskip_ref, ng_ref, expand_ref, y_ref, state_ref, yacc_ref, yb_ref):
    phase = pl.program_id(1)
    step = pl.program_id(2)

    @pl.when(step == 0)
    def _():
        state_ref[...] = jnp.zeros(state_ref.shape, F32)

    xc = xc_ref[...]
    dtv = _softplus(dt_ref[...] + bias_row_ref[...])
    a = dtv * (-jnp.exp(alog_row_ref[...]))
    used = 2 * N_SSM_HEADS
    dtvT = _softplus(dtT_ref[0:used, :] + bias_col_ref[0:used, :])
    aT = dtvT * (-jnp.exp(alog_col_ref[0:used, :]))
    li = lax.broadcasted_iota(jnp.int32, (SSD_CHUNK, SSD_CHUNK), 0)
    si = lax.broadcasted_iota(jnp.int32, (SSD_CHUNK, SSD_CHUNK), 1)
    lower = (li >= si).astype(BF16)
    upper = (li <= si).astype(BF16)
    cs = sum(jnp.dot(lower, t, preferred_element_type=F32) for t in _bf16_terms(a))
    csT = sum(jnp.dot(t, upper, preferred_element_type=F32) for t in _bf16_terms(aT))
    chunk_decay = jnp.exp(cs[SSD_CHUNK - 1:SSD_CHUNK, :])

    def decay_row(direction):
        return sum(jnp.dot(t, expand_ref[direction], preferred_element_type=F32) for t in _bf16_terms(chunk_decay))

    @pl.when(phase == 0)
    def _():
        chunk = N_CHUNKS - 1 - step
        _ssd_direction(xc, dtvT, cs, csT, a, aT, decay_row(1), state_ref, yacc_ref, N_SSM_HEADS, False)
        yb_ref[chunk] = yacc_ref[...].astype(BF16)

    @pl.when(phase == 1)
    def _():
        _ssd_direction(xc, dtvT, cs, csT, a, aT, decay_row(0), state_ref, yacc_ref, 0, True)
        xs = xc[:, :D_SSM].astype(F32)
        y = yacc_ref[...] + yb_ref[step].astype(F32) + xs * dskip_ref[...]
        z = z_ref[...].astype(F32)
        y = y * (z * _sigmoid(z))
        for g in range(N_BC_GROUPS):
            gs = slice(g * GROUP_WIDTH, (g + 1) * GROUP_WIDTH)
            yg = y[:, gs]
            ms = jnp.mean(yg * yg, axis=-1, keepdims=True)
            y_ref[:, gs] = (yg * lax.rsqrt(ms + EPS) * ng_ref[:, gs]).astype(BF16)


def _head_expansion():
    col = lax.broadcasted_iota(jnp.int32, (2, DT_PAD, D_SSM), 1)
    head = lax.broadcasted_iota(jnp.int32, (2, DT_PAD, D_SSM), 2) // SSM_HEAD_DIM
    direction = lax.broadcasted_iota(jnp.int32, (2, DT_PAD, D_SSM), 0)
    return (col == direction * N_SSM_HEADS + head).astype(BF16)


def _ssd(xconv, dt, dtT, z, alog_row, bias_row, alog_col, bias_col, dskip_row, norm_g):
    def chunk_of(p, s):
        return p * s + (1 - p) * (N_CHUNKS - 1 - s)

    rows = lambda b, p, s: (b * N_CHUNKS + chunk_of(p, s), 0)
    cols = lambda b, p, s: (0, b * N_CHUNKS + chunk_of(p, s))
    fwd_only = lambda b, p, s: (b * N_CHUNKS + p * s, 0)
    const = lambda b, p, s: (0, 0)
    return pl.pallas_call(
        _ssd_kernel,
        grid=(BATCH, 2, N_CHUNKS),
        in_specs=[
            pl.BlockSpec((SSD_CHUNK, D_CONV), rows),
            pl.BlockSpec((SSD_CHUNK, DT_PAD), rows),
            pl.BlockSpec((DT_PAD, SSD_CHUNK), cols),
            pl.BlockSpec((SSD_CHUNK, D_SSM), fwd_only),
            pl.BlockSpec((1, DT_PAD), const),
            pl.BlockSpec((1, DT_PAD), const),
            pl.BlockSpec((DT_PAD, 1), const),
            pl.BlockSpec((DT_PAD, 1), const),
            pl.BlockSpec((1, D_SSM), const),
            pl.BlockSpec((1, D_SSM), const),
            pl.BlockSpec((2, DT_PAD, D_SSM), lambda b, p, s: (0, 0, 0)),
        ],
        out_specs=pl.BlockSpec((SSD_CHUNK, D_SSM), fwd_only),
        out_shape=jax.ShapeDtypeStruct((TOKENS, D_SSM), BF16),
        scratch_shapes=[
            pltpu.VMEM((N_BC_GROUPS, D_STATE, GROUP_WIDTH), F32),
            pltpu.VMEM((SSD_CHUNK, D_SSM), F32),
            pltpu.VMEM((N_CHUNKS, SSD_CHUNK, D_SSM), BF16),
        ],
        compiler_params=_cparams(("arbitrary", "arbitrary", "arbitrary")),
        name="ssd_scan",
    )(xconv, dt, dtT, z, alog_row, bias_row, alog_col, bias_col, dskip_row, norm_g, _head_expansion())


def _outproj_kernel(x_ref, yf_ref, ys_ref, wf_ref, ws_ref, gate_ref, g_ref, shift_ref, scale_ref,
                    x1_ref, h2_ref, h2T_ref):
    mix = jnp.dot(yf_ref[...], wf_ref[...], preferred_element_type=F32)
    mix = mix + jnp.dot(ys_ref[...], ws_ref[...], preferred_element_type=F32)
    x1 = x_ref[...] + gate_ref[0] * mix
    x1_ref[...] = x1
    h2 = _modulated_norm(x1, g_ref[...], shift_ref[0], scale_ref[0])
    h2_ref[...] = h2.astype(BF16)
    h2T_ref[...] = h2.T.astype(BF16)


def _outproj(x2d, yf, ys, wf, ws, gate, gain, shift, scale):
    tiles_per_batch = SEQ // OUT_TM
    row = lambda i: (i, 0)
    const = lambda i: (0, 0)
    per_batch = lambda i: (i // tiles_per_batch, 0, 0)
    return pl.pallas_call(
        _outproj_kernel,
        grid=(TOKENS // OUT_TM,),
        in_specs=[
            pl.BlockSpec((OUT_TM, D_MODEL), row),
            pl.BlockSpec((OUT_TM, D_FNET), row),
            pl.BlockSpec((OUT_TM, D_SSM), row),
            pl.BlockSpec(wf.shape, const),
            pl.BlockSpec(ws.shape, const),
            pl.BlockSpec((1, 1, D_MODEL), per_batch),
            pl.BlockSpec((1, D_MODEL), const),
            pl.BlockSpec((1, 1, D_MODEL), per_batch),
            pl.BlockSpec((1, 1, D_MODEL), per_batch),
        ],
        out_specs=[pl.BlockSpec((OUT_TM, D_MODEL), row), pl.BlockSpec((OUT_TM, D_MODEL), row),
                   pl.BlockSpec((D_MODEL, OUT_TM), lambda i: (0, i))],
        out_shape=[jax.ShapeDtypeStruct((TOKENS, D_MODEL), F32), jax.ShapeDtypeStruct((TOKENS, D_MODEL), BF16),
                   jax.ShapeDtypeStruct((D_MODEL, TOKENS), BF16)],
        compiler_params=_cparams(("arbitrary",)),
        name="outproj",
    )(x2d, yf, ys, wf, ws, gate, gain, shift, scale)


def _extract_topk(x, k, break_ties):
    rows = x.shape[0]
    iota = lax.broadcasted_iota(jnp.int32, x.shape, 0).astype(F32)
    rank = jnp.full(x.shape, float(k), F32)
    work = x
    vals = []
    for r in range(k):
        m = jnp.max(work, axis=0, keepdims=True)
        sel = work == m
        if break_ties:
            first = jnp.min(jnp.where(sel, iota, float(rows)), axis=0, keepdims=True)
            sel = iota == first
        rank = jnp.where(sel, float(r), rank)
        work = jnp.where(sel, -jnp.inf, work)
        vals.append(m)
    return jnp.concatenate(vals, axis=0), rank


def _candidate_width(r):
    return PEER_TOPK // (r + 1)


def _candidates(v1, v2):
    sub = lax.broadcasted_iota(jnp.int32, (SUBLANES, v1.shape[1]), 0)
    blocks = [v1[0:1] + v2, v1[1:2] + v2[0:SUBLANES]]
    for r in range(2, SUBLANES):
        blocks.append(jnp.where(sub < _candidate_width(r), v1[r:r + 1] + v2[0:SUBLANES], -jnp.inf))
    blocks.append(v1[SUBLANES:] + v2[0:1])
    return jnp.concatenate(blocks, axis=0)


def _row_counts(chosen):
    counts = [jnp.sum(chosen[0:PEER_TOPK], axis=0, keepdims=True)]
    for r in range(1, SUBLANES):
        lo = PEER_TOPK + (r - 1) * SUBLANES
        counts.append(jnp.sum(chosen[lo:lo + SUBLANES], axis=0, keepdims=True))
    tail = chosen[PEER_TOPK + (SUBLANES - 1) * SUBLANES:]
    counts.extend(tail[j:j + 1] for j in range(SUBLANES))
    return counts


def _plan_head(s1, s2, break_ties):
    v1, r1 = _extract_topk(s1, PEER_TOPK, break_ties)
    v2, r2 = _extract_topk(s2, PEER_TOPK, break_ties)
    cand = _candidates(v1, v2)
    _, crank = _extract_topk(cand, PEER_TOPK, break_ties)
    chosen = (crank < float(PEER_TOPK)).astype(F32)
    counts = _row_counts(chosen)
    top = v1[0:1] + v2[0:1]
    zsum = jnp.sum(chosen * jnp.exp(cand - top), axis=0, keepdims=True)
    r1b = r1.astype(BF16)
    n1 = jnp.zeros(s1.shape, BF16)
    for r in range(PEER_TOPK):
        n1 = jnp.where(r1b == float(r), counts[r].astype(BF16), n1)
    e1 = jnp.exp(s1 - v1[0:1]) / zsum
    e2 = jnp.exp(s2 - v2[0:1])
    ranked = (r1 < float(PEER_TOPK)).astype(F32) + (r2 < float(PEER_TOPK)).astype(F32)
    total = jnp.sum(ranked, axis=0, keepdims=True) + jnp.sum(chosen, axis=0, keepdims=True)
    clean = jnp.min(jnp.where(total == 3.0 * PEER_TOPK, 1.0, 0.0)) > 0.5
    return r2, e2, n1.astype(F32), e1, clean


def _twin_bf16_words(x):
    hi = pltpu.bitcast(x.astype(BF16).astype(F32), jnp.uint32)
    return hi | (hi >> 16)


def _plan_kernel(h2_ref, wq_ref, keys_ref, c2_ref, e2_ref, n1_ref, e1_ref, q_ref):
    q_ref[...] = jnp.dot(h2_ref[...], wq_ref[...], preferred_element_type=F32).astype(BF16)

    def per_head(h, carry):
        def scores(side):
            col = pl.multiple_of((2 * h + side) * PEER_HALF, PEER_HALF)
            qh = q_ref[:, pl.ds(col, PEER_HALF)]
            return lax.dot_general(keys_ref[2 * h + side], qh, (((1,), (1,)), ((), ())), preferred_element_type=F32)

        s1 = scores(0)
        s2 = scores(1)

        def store(r2, e2, n1, e1):
            c2_ref[h] = r2.astype(BF16)
            e2_ref[h] = e2.astype(BF16)
            for c in range(PLAN_TT // LANES):
                cols = slice(c * LANES, (c + 1) * LANES)
                n1_ref[h, c] = _twin_bf16_words(n1[:, cols])
                e1_ref[h, c] = _twin_bf16_words(e1[:, cols])

        r2, e2, n1, e1, clean = _plan_head(s1, s2, break_ties=False)
        store(r2, e2, n1, e1)

        @pl.when(jnp.logical_not(clean))
        def _():
            store(*_plan_head(s1, s2, break_ties=True)[:4])

        return carry

    lax.fori_loop(0, PEER_HEADS, per_head, 0)


def _plan(h2, wq, keys):
    blk = lambda i: (0, 0, i)
    shp = (PEER_HEADS, PEER_KEYS, TOKENS)
    lane_blk = lambda i: (0, i, 0, 0)
    lane_shp = (PEER_HEADS, TOKENS // LANES, PEER_KEYS, LANES)
    lane_spec = pl.BlockSpec((PEER_HEADS, PLAN_TT // LANES, PEER_KEYS, LANES), lane_blk)
    return pl.pallas_call(
        _plan_kernel,
        grid=(TOKENS // PLAN_TT,),
        in_specs=[
            pl.BlockSpec((PLAN_TT, D_MODEL), lambda i: (i, 0)),
            pl.BlockSpec(wq.shape, lambda i: (0, 0)),
            pl.BlockSpec(keys.shape, lambda i: (0, 0, 0)),
        ],
        out_specs=[pl.BlockSpec((PEER_HEADS, PEER_KEYS, PLAN_TT), blk)] * 2 + [lane_spec] * 2,
        out_shape=[
            jax.ShapeDtypeStruct(shp, BF16),
            jax.ShapeDtypeStruct(shp, BF16),
            jax.ShapeDtypeStruct(lane_shp, jnp.uint32),
            jax.ShapeDtypeStruct(lane_shp, jnp.uint32),
        ],
        scratch_shapes=[pltpu.VMEM((PLAN_TT, 2 * PEER_HEADS * PEER_HALF), BF16)],
        compiler_params=_cparams(("arbitrary",)),
        name="peer_plan",
    )(h2, wq, keys)


def _gelu_bf16(x):
    half_x = (0.5 * x).astype(BF16)
    return half_x + half_x * lax.erf(x * (1.0 / math.sqrt(2.0))).astype(BF16)


DENSE_NE = PEER_EXPERTS // DENSE_TE
DENSE_NT = TOKENS // DENSE_TT
DENSE_TILES = DENSE_NT * DENSE_NE
DENSE_KEYS_PER_TILE = DENSE_TE // PEER_KEYS


def _dense_stage_body(f, h2_ref, down_ref, upT_ref, c2_ref, e2_ref, n1_ref, e1_ref, acc_ref, act_w, act_r):
    gate_tile = jnp.clip(f - 1, 0, DENSE_TILES - 1) % DENSE_NE

    def act_piece(q, c):
        rows = slice(q * DENSE_SUB, (q + 1) * DENSE_SUB)
        cols = slice(c * DENSE_CHUNK, (c + 1) * DENSE_CHUNK)
        act_w[rows, cols] = jnp.dot(down_ref[rows, :], h2_ref[:, cols], preferred_element_type=F32)

    def gate_piece(k, c):
        i1 = gate_tile * DENSE_KEYS_PER_TILE + k
        cols = slice(c * DENSE_CHUNK, (c + 1) * DENSE_CHUNK)
        rows = slice(k * PEER_KEYS, (k + 1) * PEER_KEYS)

        def row_tile(ref, h):
            tiles = []
            for t in range(DENSE_CHUNK // LANES):
                lane_tile = c * (DENSE_CHUNK // LANES) + t
                words = jnp.broadcast_to(ref[h, lane_tile, pl.ds(i1, 1), :], (SUBLANES, LANES))
                tiles.append(jnp.concatenate([pltpu.bitcast(words, BF16)] * (PEER_KEYS // BF16_ROWS), axis=0))
            return jnp.concatenate(tiles, axis=1)

        gates = jnp.zeros((PEER_KEYS, DENSE_CHUNK), BF16)
        for h in range(PEER_HEADS):
            hit = c2_ref[h, :, cols] < row_tile(n1_ref, h)
            gates = gates + jnp.where(hit, e2_ref[h, :, cols], jnp.zeros((), BF16)) * row_tile(e1_ref, h)
        return gates * _gelu_bf16(act_r[rows, cols])

    keys_per_sub = DENSE_SUB // PEER_KEYS
    n_sub = DENSE_TE // DENSE_SUB
    for c in range(DENSE_TT // DENSE_CHUNK):
        cols = slice(c * DENSE_CHUNK, (c + 1) * DENSE_CHUNK)
        part = None
        for j in range(n_sub):
            w = jnp.concatenate([gate_piece(keys_per_sub * j + k, c) for k in range(keys_per_sub)], axis=0)
            d = jnp.dot(upT_ref[:, j * DENSE_SUB:(j + 1) * DENSE_SUB], w, preferred_element_type=F32)
            part = d if part is None else part + d
            act_piece(j, c)
        acc_ref[:, cols] += part


def _dense_kernel(h2_ref, down_ref, upT_ref, c2_ref, e2_ref, n1_ref, e1_ref, x1_ref, gate_ref, g_ref,
                  o_ref, acc_ref, act0_ref, act1_ref):
    f = pl.program_id(0)
    out_tile = jnp.clip(f - 1, 0, DENSE_TILES - 1) % DENSE_NE

    @pl.when(f == 0)
    def _():
        act1_ref[...] = jnp.zeros(act1_ref.shape, F32)

    @pl.when(out_tile == 0)
    def _():
        acc_ref[...] = jnp.zeros(acc_ref.shape, F32)

    stage = functools.partial(_dense_stage_body, f, h2_ref, down_ref, upT_ref, c2_ref, e2_ref, n1_ref, e1_ref,
                              acc_ref)

    @pl.when(f % 2 == 0)
    def _():
        stage(act0_ref, act1_ref)

    @pl.when(f % 2 == 1)
    def _():
        stage(act1_ref, act0_ref)

    @pl.when(jnp.logical_and(out_tile == DENSE_NE - 1, f >= 1))
    def _():
        x2 = x1_ref[...] + gate_ref[0] * acc_ref[...].T
        ms = jnp.mean(x2 * x2, axis=-1, keepdims=True)
        o_ref[...] = x2 * lax.rsqrt(ms + EPS) * g_ref[...]


def _dense(h2T, down, upT, c2, e2, n1, e1, x1, gate, final_g):
    tiles_per_batch = SEQ // DENSE_TT

    def tile(f, lag):
        return jnp.clip(f - lag, 0, DENSE_TILES - 1)

    plan = lambda f: (0, tile(f, 1) // DENSE_NE, 0, 0)
    out_tok = lambda f: (tile(f, 1) // DENSE_NE, 0)
    plan_blk = (PEER_HEADS, DENSE_TT // LANES, PEER_KEYS, LANES)
    rank_map = lambda f: (0, 0, tile(f, 1) // DENSE_NE)
    rank_blk = (PEER_HEADS, PEER_KEYS, DENSE_TT)
    return pl.pallas_call(
        _dense_kernel,
        grid=(DENSE_TILES + 1,),
        in_specs=[
            pl.BlockSpec((D_MODEL, DENSE_TT), lambda f: (0, tile(f, 0) // DENSE_NE)),
            pl.BlockSpec((DENSE_TE, D_MODEL), lambda f: (tile(f, 0) % DENSE_NE, 0)),
            pl.BlockSpec((D_MODEL, DENSE_TE), lambda f: (0, tile(f, 1) % DENSE_NE)),
            pl.BlockSpec(rank_blk, rank_map),
            pl.BlockSpec(rank_blk, rank_map),
            pl.BlockSpec(plan_blk, plan),
            pl.BlockSpec(plan_blk, plan),
            pl.BlockSpec((DENSE_TT, D_MODEL), out_tok),
            pl.BlockSpec((1, 1, D_MODEL), lambda f: (tile(f, 1) // DENSE_NE // tiles_per_batch, 0, 0)),
            pl.BlockSpec((1, D_MODEL), lambda f: (0, 0)),
        ],
        out_specs=pl.BlockSpec((DENSE_TT, D_MODEL), out_tok),
        out_shape=jax.ShapeDtypeStruct((TOKENS, D_MODEL), F32),
        scratch_shapes=[
            pltpu.VMEM((D_MODEL, DENSE_TT), F32),
            pltpu.VMEM((DENSE_TE, DENSE_TT), F32),
            pltpu.VMEM((DENSE_TE, DENSE_TT), F32),
        ],
        compiler_params=_cparams(("arbitrary",)),
        name="peer_dense",
    )(h2T, down, upT, c2, e2, n1, e1, x1, gate, final_g)


def _transpose_kernel(x_ref, o_ref):
    o_ref[...] = x_ref[...].T.astype(BF16)


def _transpose_table(t):
    rows, cols = t.shape
    tr = 512
    return pl.pallas_call(
        _transpose_kernel,
        grid=(rows // tr,),
        in_specs=[pl.BlockSpec((tr, cols), lambda i: (i, 0))],
        out_specs=pl.BlockSpec((cols, tr), lambda i: (0, i)),
        out_shape=jax.ShapeDtypeStruct((cols, rows), BF16),
        compiler_params=_cparams(("arbitrary",)),
        name="table_transpose",
    )(t)


def _position_dft_tables():
    radix = 64
    r = lax.broadcasted_iota(jnp.int32, (radix, SEQ), 0)
    k = lax.broadcasted_iota(jnp.int32, (radix, SEQ), 1)
    coarse = ((r * k) % radix).astype(F32) * (2.0 * math.pi / radix)
    fine = ((r * k) % SEQ).astype(F32) * (2.0 * math.pi / SEQ)
    scale = 1.0 / math.sqrt(SEQ)
    ca, sa = jnp.cos(coarse)[:, None, :] * scale, jnp.sin(coarse)[:, None, :] * scale
    cb, sb = jnp.cos(fine)[None, :, :], jnp.sin(fine)[None, :, :]
    cos_tab = (ca * cb - sa * sb).reshape(SEQ, SEQ).astype(BF16)
    sin_tab = (sa * cb + ca * sb).reshape(SEQ, SEQ).astype(BF16)
    return cos_tab, sin_tab


def _channel_dft_table():
    j = lax.broadcasted_iota(jnp.int32, (FNET_GROUP, FNET_GROUP), 0)
    k = lax.broadcasted_iota(jnp.int32, (FNET_GROUP, FNET_GROUP), 1)
    ang = ((j * k) % FNET_GROUP).astype(F32) * (2.0 * math.pi / FNET_GROUP)
    scale = 1.0 / math.sqrt(FNET_GROUP)
    return jnp.concatenate([jnp.cos(ang) * scale, jnp.sin(ang) * scale], axis=1).astype(BF16)


def _pad_lanes(v, width):
    return jnp.pad(v, ((0, 0), (0, width - v.shape[1])))


def kernel(x, c, w_ada, b_ada, norm_mix_g, w_in, conv_w, conv_b, a_log_fwd, a_log_bwd, dt_bias_fwd, dt_bias_bwd, d_skip, ssm_norm_g, w_out, norm_ffn_g, w_query, sub_keys, expert_down, expert_up, final_norm_g):
    assert w_ada.shape[0] == 1, "single-layer problem: the final RMSNorm is fused into the PEER kernel"
    xt = x.reshape(TOKENS, D_MODEL)
    c_pad = jnp.pad(c, ((0, SUBLANES - BATCH), (0, 0)))
    cs_tab, ss_tab = _position_dft_tables()
    csc_tab = _channel_dft_table()
    for layer in range(1):
        mod = _adaln(c_pad, w_ada[layer], b_ada[layer][None, :])[:BATCH]
        shift_m, scale_m, gate_m, shift_f, scale_f, gate_f = [
            m.reshape(BATCH, 1, D_MODEL) for m in jnp.split(mod, 6, axis=-1)]

        w = w_in[layer]
        wf = w[:, :D_FNET].astype(BF16)
        wz = w[:, D_FNET:D_FNET + D_SSM].astype(BF16)
        wx = w[:, D_FNET + D_SSM:D_FNET + D_SSM + D_CONV].astype(BF16)
        wdt = _pad_lanes(w[:, D_FNET + D_SSM + D_CONV:], DT_PAD).astype(BF16)
        uc, us, z, xbc, dt, dtT = _inproj(xt, norm_mix_g[layer][None, :], shift_m, scale_m,
                                          wf, wz, wx, wdt, wdt.T, csc_tab)
        y_fnet = _dft(cs_tab, ss_tab, uc, us)

        xconv = _conv(xbc.reshape(BATCH, SEQ, D_CONV), conv_w[layer], conv_b[layer][None, :])
        alog_row = _pad_lanes(jnp.concatenate([a_log_fwd[layer], a_log_bwd[layer]])[None, :], DT_PAD)
        bias_row = _pad_lanes(jnp.concatenate([dt_bias_fwd[layer], dt_bias_bwd[layer]])[None, :], DT_PAD)
        dskip_row = jnp.repeat(d_skip[layer], SSM_HEAD_DIM)[None, :]
        y_ssm = _ssd(xconv.reshape(TOKENS, D_CONV), dt, dtT, z, alog_row, bias_row, alog_row.T, bias_row.T,
                     dskip_row, ssm_norm_g[layer][None, :])

        wo = w_out[layer].astype(BF16)
        x1, h2, h2T = _outproj(xt, y_fnet, y_ssm, wo[:D_FNET], wo[D_FNET:], gate_m,
                          norm_ffn_g[layer][None, :], shift_f, scale_f)

        keys = sub_keys[layer].reshape(2 * PEER_HEADS, PEER_KEYS, PEER_HALF).astype(BF16)
        c2, e2, n1, e1 = _plan(h2, w_query[layer].astype(BF16), keys)
        xt = _dense(h2T, expert_down[layer].astype(BF16), _transpose_table(expert_up[layer]),
                    c2, e2, n1, e1, x1, gate_f, final_norm_g[None, :])
    return xt.reshape(BATCH, SEQ, D_MODEL)
```

```python
import functools
import math

import jax
import jax.numpy as jnp
from jax import lax
from jax.experimental import pallas as pl
from jax.experimental.pallas import tpu as pltpu

F32 = jnp.float32
BF16 = jnp.bfloat16
HIGHEST = lax.Precision.HIGHEST

D_MODEL = 1024
BATCH = 4
SEQ = 4096
TOKENS = BATCH * SEQ
D_MIX = 2 * D_MODEL
D_FNET = D_MIX // 4
N_FNET_GROUPS = 4
FNET_GROUP = D_FNET // N_FNET_GROUPS
D_SSM = D_MIX - D_FNET
SSM_HEAD_DIM = 64
N_SSM_HEADS = D_SSM // SSM_HEAD_DIM
N_BC_GROUPS = 4
HEADS_PER_GROUP = N_SSM_HEADS // N_BC_GROUPS
D_STATE = 128
CONV_WIDTH = 5
SSD_CHUNK = 128
N_CHUNKS = SEQ // SSD_CHUNK
D_BC = N_BC_GROUPS * D_STATE
D_CONV = D_SSM + 2 * D_BC
GROUP_WIDTH = D_SSM // N_BC_GROUPS
PEER_HEADS = 8
PEER_KEYS = 128
PEER_EXPERTS = PEER_KEYS * PEER_KEYS
PEER_HALF = 128
PEER_TOPK = 16
EPS = 1e-6

LANES = 128
SUBLANES = 8
BF16_ROWS = 16
DT_PAD = LANES
VMEM_LIMIT = 56 * 1024 * 1024

ADA_TN = 512
PROJ_TM = 512
DFT_TM = 512
CONV_TC = 256
CONV_TR = 512
CONV_HALO = SUBLANES
OUT_TM = 512
PLAN_TT = 256
DENSE_TT = 512
DENSE_TE = 2048
DENSE_SUB = 256
DENSE_CHUNK = 256


def _cparams(sem):
    return pltpu.CompilerParams(dimension_semantics=sem, vmem_limit_bytes=VMEM_LIMIT)


def _sigmoid(x):
    return 1.0 / (1.0 + jnp.exp(-x))


def _bf16_terms(x):
    terms = []
    rest = x
    for _ in range(3):
        term = rest.astype(BF16)
        terms.append(term)
        rest = rest - term.astype(F32)
    return terms


def _softplus(x):
    return jnp.maximum(x, 0.0) + jnp.log(1.0 + jnp.exp(-jnp.abs(x)))


def _adaln_kernel(c_ref, w_ref, b_ref, o_ref):
    c = c_ref[...]
    ca = c * _sigmoid(c)
    o_ref[...] = jnp.dot(ca, w_ref[...], preferred_element_type=F32, precision=HIGHEST) + b_ref[...]


def _adaln(c_pad, w_ada, b_ada):
    n = w_ada.shape[1]
    return pl.pallas_call(
        _adaln_kernel,
        grid=(n // ADA_TN,),
        in_specs=[
            pl.BlockSpec((SUBLANES, D_MODEL), lambda j: (0, 0)),
            pl.BlockSpec((D_MODEL, ADA_TN), lambda j: (0, j)),
            pl.BlockSpec((1, ADA_TN), lambda j: (0, j)),
        ],
        out_specs=pl.BlockSpec((SUBLANES, ADA_TN), lambda j: (0, j)),
        out_shape=jax.ShapeDtypeStruct((SUBLANES, n), F32),
        compiler_params=_cparams(("arbitrary",)),
        name="adaln",
    )(c_pad, w_ada, b_ada)


def _modulated_norm(x, gain, shift, scale):
    ms = jnp.mean(x * x, axis=-1, keepdims=True)
    xn = x * lax.rsqrt(ms + EPS) * gain
    return xn * (1.0 + scale) + shift


def _inproj_kernel(x_ref, g_ref, shift_ref, scale_ref, wf_ref, wz_ref, wx_ref, wdt_ref, wdtT_ref, csc_ref,
                   uc_ref, us_ref, z_ref, xbc_ref, dt_ref, dtT_ref):
    h = _modulated_norm(x_ref[...], g_ref[...], shift_ref[0], scale_ref[0])
    hb = h.astype(BF16)
    fb = jnp.dot(hb, wf_ref[...], preferred_element_type=F32).astype(BF16)
    for g in range(N_FNET_GROUPS):
        sl = slice(g * FNET_GROUP, (g + 1) * FNET_GROUP)
        u = jnp.dot(fb[:, sl], csc_ref[...], preferred_element_type=F32)
        uc_ref[:, sl] = u[:, :FNET_GROUP].astype(BF16)
        us_ref[:, sl] = u[:, FNET_GROUP:].astype(BF16)
    z_ref[...] = jnp.dot(hb, wz_ref[...], preferred_element_type=F32).astype(BF16)
    xbc_ref[...] = jnp.dot(hb, wx_ref[...], preferred_element_type=F32).astype(BF16)
    dt_ref[...] = jnp.dot(hb, wdt_ref[...], preferred_element_type=F32)
    dtT_ref[...] = lax.dot_general(wdtT_ref[...], hb, (((1,), (1,)), ((), ())), preferred_element_type=F32)


def _inproj(x2d, gain, shift, scale, wf, wz, wx, wdt, wdtT, csc):
    tiles_per_batch = SEQ // PROJ_TM
    row = lambda i: (i, 0)
    const = lambda i: (0, 0)
    per_batch = lambda i: (i // tiles_per_batch, 0, 0)
    return pl.pallas_call(
        _inproj_kernel,
        grid=(TOKENS // PROJ_TM,),
        in_specs=[
            pl.BlockSpec((PROJ_TM, D_MODEL), row),
            pl.BlockSpec((1, D_MODEL), const),
            pl.BlockSpec((1, 1, D_MODEL), per_batch),
            pl.BlockSpec((1, 1, D_MODEL), per_batch),
            pl.BlockSpec(wf.shape, const),
            pl.BlockSpec(wz.shape, const),
            pl.BlockSpec(wx.shape, const),
            pl.BlockSpec(wdt.shape, const),
            pl.BlockSpec(wdtT.shape, const),
            pl.BlockSpec(csc.shape, const),
        ],
        out_specs=[
            pl.BlockSpec((PROJ_TM, D_FNET), row),
            pl.BlockSpec((PROJ_TM, D_FNET), row),
            pl.BlockSpec((PROJ_TM, D_SSM), row),
            pl.BlockSpec((PROJ_TM, D_CONV), row),
            pl.BlockSpec((PROJ_TM, DT_PAD), row),
            pl.BlockSpec((DT_PAD, PROJ_TM), lambda i: (0, i)),
        ],
        out_shape=[
            jax.ShapeDtypeStruct((TOKENS, D_FNET), BF16),
            jax.ShapeDtypeStruct((TOKENS, D_FNET), BF16),
            jax.ShapeDtypeStruct((TOKENS, D_SSM), BF16),
            jax.ShapeDtypeStruct((TOKENS, D_CONV), BF16),
            jax.ShapeDtypeStruct((TOKENS, DT_PAD), F32),
            jax.ShapeDtypeStruct((DT_PAD, TOKENS), F32),
        ],
        compiler_params=_cparams(("arbitrary",)),
        name="inproj",
    )(x2d, gain, shift, scale, wf, wz, wx, wdt, wdtT, csc)


def _dft_kernel(cs_ref, ss_ref, uc_ref, us_ref, o_ref):
    y = jnp.dot(cs_ref[...], uc_ref[...], preferred_element_type=F32)
    y = y - jnp.dot(ss_ref[...], us_ref[...], preferred_element_type=F32)
    o_ref[...] = y.astype(BF16)


def _dft(cs, ss, uc, us):
    mt = SEQ // DFT_TM
    return pl.pallas_call(
        _dft_kernel,
        grid=(BATCH, mt),
        in_specs=[
            pl.BlockSpec((DFT_TM, SEQ), lambda b, m: (m, 0)),
            pl.BlockSpec((DFT_TM, SEQ), lambda b, m: (m, 0)),
            pl.BlockSpec((SEQ, D_FNET), lambda b, m: (b, 0)),
            pl.BlockSpec((SEQ, D_FNET), lambda b, m: (b, 0)),
        ],
        out_specs=pl.BlockSpec((DFT_TM, D_FNET), lambda b, m: (b * mt + m, 0)),
        out_shape=jax.ShapeDtypeStruct((TOKENS, D_FNET), BF16),
        compiler_params=_cparams(("arbitrary", "arbitrary")),
        name="fnet_dft",
    )(cs, ss, uc, us)


def _conv_kernel(x_ref, w_ref, b_ref, o_ref, pad_ref):
    zeros = jnp.zeros((CONV_HALO, CONV_TC), F32)
    pad_ref[0:CONV_HALO, :] = zeros
    pad_ref[SEQ + CONV_HALO:SEQ + 2 * CONV_HALO, :] = zeros
    pad_ref[CONV_HALO:SEQ + CONV_HALO, :] = x_ref[0].astype(F32)
    w = w_ref[...]
    bias = b_ref[...]
    first = CONV_HALO - CONV_WIDTH // 2
    for i in range(SEQ // CONV_TR):
        acc = jnp.broadcast_to(bias, (CONV_TR, CONV_TC))
        for k in range(CONV_WIDTH):
            lo = first + k + i * CONV_TR
            acc = acc + pad_ref[lo:lo + CONV_TR, :] * w[k:k + 1, :]
        o_ref[0, i * CONV_TR:(i + 1) * CONV_TR, :] = (acc * _sigmoid(acc)).astype(BF16)


def _conv(xbc3, conv_w, conv_b):
    return pl.pallas_call(
        _conv_kernel,
        grid=(BATCH, D_CONV // CONV_TC),
        in_specs=[
            pl.BlockSpec((1, SEQ, CONV_TC), lambda b, j: (b, 0, j)),
            pl.BlockSpec((CONV_WIDTH, CONV_TC), lambda b, j: (0, j)),
            pl.BlockSpec((1, CONV_TC), lambda b, j: (0, j)),
        ],
        out_specs=pl.BlockSpec((1, SEQ, CONV_TC), lambda b, j: (b, 0, j)),
        out_shape=jax.ShapeDtypeStruct((BATCH, SEQ, D_CONV), BF16),
        scratch_shapes=[pltpu.VMEM((SEQ + 2 * CONV_HALO, CONV_TC), F32)],
        compiler_params=_cparams(("arbitrary", "arbitrary")),
        name="conv_silu",
    )(xbc3, conv_w, conv_b)


def _ssd_direction(xc, dtvT, cs, csT, a, aT, decay_row, state_ref, yacc_ref, head_base, forward):
    li = lax.broadcasted_iota(jnp.int32, (SSD_CHUNK, SSD_CHUNK), 0)
    si = lax.broadcasted_iota(jnp.int32, (SSD_CHUNK, SSD_CHUNK), 1)
    low_half = si < SSM_HEAD_DIM
    if forward:
        u, uT = cs, csT
        mask = li >= si
    else:
        u, uT = cs - a, csT - aT
        mask = si >= li
    pair = 2 * SSM_HEAD_DIM
    for g in range(N_BC_GROUPS):
        bc = xc[:, D_SSM + g * D_STATE:D_SSM + (g + 1) * D_STATE]
        cc = xc[:, D_SSM + D_BC + g * D_STATE:D_SSM + D_BC + (g + 1) * D_STATE]
        gmat = lax.dot_general(cc, bc, (((1,), (1,)), ((), ())), preferred_element_type=F32)
        bcT = bc.astype(F32).T
        ccf = cc.astype(F32)
        for pr in range(HEADS_PER_GROUP // 2):
            h0 = g * HEADS_PER_GROUP + 2 * pr
            xcols = slice(h0 * SSM_HEAD_DIM, h0 * SSM_HEAD_DIM + pair)
            scols = slice(pr * pair, (pr + 1) * pair)
            xpair = xc[:, xcols]
            spair = state_ref[g, :, scols]
            rhs = jnp.concatenate([xpair, spair.astype(BF16)], axis=0)
            ys, upds = [], []
            for j in range(2):
                hh = head_base + h0 + j
                ub = jnp.broadcast_to(u[:, hh:hh + 1], (SSD_CHUNK, SSD_CHUNK))
                urow = uT[hh:hh + 1, :]
                dtrow = dtvT[hh:hh + 1, :]
                tot = cs[SSD_CHUNK - 1:SSD_CHUNK, hh:hh + 1]
                if forward:
                    diff = ub - urow
                    cscale = jnp.exp(ub)
                    wrow = jnp.exp(tot - urow) * dtrow
                else:
                    diff = urow - ub
                    cscale = jnp.exp(tot - ub)
                    wrow = jnp.exp(urow) * dtrow
                lmat = jnp.where(mask, jnp.exp(diff), 0.0)
                lhs = jnp.concatenate([(gmat * lmat * dtrow).astype(BF16), (ccf * cscale).astype(BF16)], axis=1)
                ys.append(jnp.dot(lhs, rhs, preferred_element_type=F32))
                upds.append(jnp.dot((bcT * wrow).astype(BF16), xpair, preferred_element_type=F32))
            yacc_ref[:, xcols] = jnp.where(low_half, ys[0], ys[1])
            state_ref[g, :, scols] = spair * decay_row[:, xcols] + jnp.where(low_half, upds[0], upds[1])


def _ssd_kernel(xc_ref, dt_ref, dtT_ref, z_ref, alog_row_ref, bias_row_ref, alog_col_ref, bias_col_ref,
                dskip_ref, ng_ref, expand_ref, y_ref, state_ref, yacc_ref, yb_ref):
    phase = pl.program_id(1)
    step = pl.program_id(2)

    @pl.when(step == 0)
    def _():
        state_ref[...] = jnp.zeros(state_ref.shape, F32)

    xc = xc_ref[...]
    dtv = _softplus(dt_ref[...] + bias_row_ref[...])
    a = dtv * (-jnp.exp(alog_row_ref[...]))
    used = 2 * N_SSM_HEADS
    dtvT = _softplus(dtT_ref[0:used, :] + bias_col_ref[0:used, :])
    aT = dtvT * (-jnp.exp(alog_col_ref[0:used, :]))
    li = lax.broadcasted_iota(jnp.int32, (SSD_CHUNK, SSD_CHUNK), 0)
    si = lax.broadcasted_iota(jnp.int32, (SSD_CHUNK, SSD_CHUNK), 1)
    lower = (li >= si).astype(BF16)
    upper = (li <= si).astype(BF16)
    cs = sum(jnp.dot(lower, t, preferred_element_type=F32) for t in _bf16_terms(a))
    csT = sum(jnp.dot(t, upper, preferred_element_type=F32) for t in _bf16_terms(aT))
    chunk_decay = jnp.exp(cs[SSD_CHUNK - 1:SSD_CHUNK, :])

    def decay_row(direction):
        return sum(jnp.dot(t, expand_ref[direction], preferred_element_type=F32) for t in _bf16_terms(chunk_decay))

    @pl.when(phase == 0)
    def _():
        chunk = N_CHUNKS - 1 - step
        _ssd_direction(xc, dtvT, cs, csT, a, aT, decay_row(1), state_ref, yacc_ref, N_SSM_HEADS, False)
        yb_ref[chunk] = yacc_ref[...].astype(BF16)

    @pl.when(phase == 1)
    def _():
        _ssd_direction(xc, dtvT, cs, csT, a, aT, decay_row(0), state_ref, yacc_ref, 0, True)
        xs = xc[:, :D_SSM].astype(F32)
        y = yacc_ref[...] + yb_ref[step].astype(F32) + xs * dskip_ref[...]
        z = z_ref[...].astype(F32)
        y = y * (z * _sigmoid(z))
        for g in range(N_BC_GROUPS):
            gs = slice(g * GROUP_WIDTH, (g + 1) * GROUP_WIDTH)
            yg = y[:, gs]
            ms = jnp.mean(yg * yg, axis=-1, keepdims=True)
            y_ref[:, gs] = (yg * lax.rsqrt(ms + EPS) * ng_ref[:, gs]).astype(BF16)


def _head_expansion():
    col = lax.broadcasted_iota(jnp.int32, (2, DT_PAD, D_SSM), 1)
    head = lax.broadcasted_iota(jnp.int32, (2, DT_PAD, D_SSM), 2) // SSM_HEAD_DIM
    direction = lax.broadcasted_iota(jnp.int32, (2, DT_PAD, D_SSM), 0)
    return (col == direction * N_SSM_HEADS + head).astype(BF16)


def _ssd(xconv, dt, dtT, z, alog_row, bias_row, alog_col, bias_col, dskip_row, norm_g):
    def chunk_of(p, s):
        return p * s + (1 - p) * (N_CHUNKS - 1 - s)

    rows = lambda b, p, s: (b * N_CHUNKS + chunk_of(p, s), 0)
    cols = lambda b, p, s: (0, b * N_CHUNKS + chunk_of(p, s))
    fwd_only = lambda b, p, s: (b * N_CHUNKS + p * s, 0)
    const = lambda b, p, s: (0, 0)
    return pl.pallas_call(
        _ssd_kernel,
        grid=(BATCH, 2, N_CHUNKS),
        in_specs=[
            pl.BlockSpec((SSD_CHUNK, D_CONV), rows),
            pl.BlockSpec((SSD_CHUNK, DT_PAD), rows),
            pl.BlockSpec((DT_PAD, SSD_CHUNK), cols),
            pl.BlockSpec((SSD_CHUNK, D_SSM), fwd_only),
            pl.BlockSpec((1, DT_PAD), const),
            pl.BlockSpec((1, DT_PAD), const),
            pl.BlockSpec((DT_PAD, 1), const),
            pl.BlockSpec((DT_PAD, 1), const),
            pl.BlockSpec((1, D_SSM), const),
            pl.BlockSpec((1, D_SSM), const),
            pl.BlockSpec((2, DT_PAD, D_SSM), lambda b, p, s: (0, 0, 0)),
        ],
        out_specs=pl.BlockSpec((SSD_CHUNK, D_SSM), fwd_only),
        out_shape=jax.ShapeDtypeStruct((TOKENS, D_SSM), BF16),
        scratch_shapes=[
            pltpu.VMEM((N_BC_GROUPS, D_STATE, GROUP_WIDTH), F32),
            pltpu.VMEM((SSD_CHUNK, D_SSM), F32),
            pltpu.VMEM((N_CHUNKS, SSD_CHUNK, D_SSM), BF16),
        ],
        compiler_params=_cparams(("arbitrary", "arbitrary", "arbitrary")),
        name="ssd_scan",
    )(xconv, dt, dtT, z, alog_row, bias_row, alog_col, bias_col, dskip_row, norm_g, _head_expansion())


def _outproj_kernel(x_ref, yf_ref, ys_ref, wf_ref, ws_ref, gate_ref, g_ref, shift_ref, scale_ref,
                    x1_ref, h2_ref, h2T_ref):
    mix = jnp.dot(yf_ref[...], wf_ref[...], preferred_element_type=F32)
    mix = mix + jnp.dot(ys_ref[...], ws_ref[...], preferred_element_type=F32)
    x1 = x_ref[...] + gate_ref[0] * mix
    x1_ref[...] = x1
    h2 = _modulated_norm(x1, g_ref[...], shift_ref[0], scale_ref[0])
    h2_ref[...] = h2.astype(BF16)
    h2T_ref[...] = h2.T.astype(BF16)


def _outproj(x2d, yf, ys, wf, ws, gate, gain, shift, scale):
    tiles_per_batch = SEQ // OUT_TM
    row = lambda i: (i, 0)
    const = lambda i: (0, 0)
    per_batch = lambda i: (i // tiles_per_batch, 0, 0)
    return pl.pallas_call(
        _outproj_kernel,
        grid=(TOKENS // OUT_TM,),
        in_specs=[
            pl.BlockSpec((OUT_TM, D_MODEL), row),
            pl.BlockSpec((OUT_TM, D_FNET), row),
            pl.BlockSpec((OUT_TM, D_SSM), row),
            pl.BlockSpec(wf.shape, const),
            pl.BlockSpec(ws.shape, const),
            pl.BlockSpec((1, 1, D_MODEL), per_batch),
            pl.BlockSpec((1, D_MODEL), const),
            pl.BlockSpec((1, 1, D_MODEL), per_batch),
            pl.BlockSpec((1, 1, D_MODEL), per_batch),
        ],
        out_specs=[pl.BlockSpec((OUT_TM, D_MODEL), row), pl.BlockSpec((OUT_TM, D_MODEL), row),
                   pl.BlockSpec((D_MODEL, OUT_TM), lambda i: (0, i))],
        out_shape=[jax.ShapeDtypeStruct((TOKENS, D_MODEL), F32), jax.ShapeDtypeStruct((TOKENS, D_MODEL), BF16),
                   jax.ShapeDtypeStruct((D_MODEL, TOKENS), BF16)],
        compiler_params=_cparams(("arbitrary",)),
        name="outproj",
    )(x2d, yf, ys, wf, ws, gate, gain, shift, scale)


def _extract_topk(x, k, break_ties):
    rows = x.shape[0]
    iota = lax.broadcasted_iota(jnp.int32, x.shape, 0).astype(F32)
    rank = jnp.full(x.shape, float(k), F32)
    work = x
    vals = []
    for r in range(k):
        m = jnp.max(work, axis=0, keepdims=True)
        sel = work == m
        if break_ties:
            first = jnp.min(jnp.where(sel, iota, float(rows)), axis=0, keepdims=True)
            sel = iota == first
        rank = jnp.where(sel, float(r), rank)
        work = jnp.where(sel, -jnp.inf, work)
        vals.append(m)
    return jnp.concatenate(vals, axis=0), rank


def _candidate_width(r):
    return PEER_TOPK // (r + 1)


def _candidates(v1, v2):
    sub = lax.broadcasted_iota(jnp.int32, (SUBLANES, v1.shape[1]), 0)
    blocks = [v1[0:1] + v2, v1[1:2] + v2[0:SUBLANES]]
    for r in range(2, SUBLANES):
        blocks.append(jnp.where(sub < _candidate_width(r), v1[r:r + 1] + v2[0:SUBLANES], -jnp.inf))
    blocks.append(v1[SUBLANES:] + v2[0:1])
    return jnp.concatenate(blocks, axis=0)


def _row_counts(chosen):
    counts = [jnp.sum(chosen[0:PEER_TOPK], axis=0, keepdims=True)]
    for r in range(1, SUBLANES):
        lo = PEER_TOPK + (r - 1) * SUBLANES
        counts.append(jnp.sum(chosen[lo:lo + SUBLANES], axis=0, keepdims=True))
    tail = chosen[PEER_TOPK + (SUBLANES - 1) * SUBLANES:]
    counts.extend(tail[j:j + 1] for j in range(SUBLANES))
    return counts


def _plan_head(s1, s2, break_ties):
    v1, r1 = _extract_topk(s1, PEER_TOPK, break_ties)
    v2, r2 = _extract_topk(s2, PEER_TOPK, break_ties)
    cand = _candidates(v1, v2)
    _, crank = _extract_topk(cand, PEER_TOPK, break_ties)
    chosen = (crank < float(PEER_TOPK)).astype(F32)
    counts = _row_counts(chosen)
    top = v1[0:1] + v2[0:1]
    zsum = jnp.sum(chosen * jnp.exp(cand - top), axis=0, keepdims=True)
    r1b = r1.astype(BF16)
    n1 = jnp.zeros(s1.shape, BF16)
    for r in range(PEER_TOPK):
        n1 = jnp.where(r1b == float(r), counts[r].astype(BF16), n1)
    e1 = jnp.exp(s1 - v1[0:1]) / zsum
    e2 = jnp.exp(s2 - v2[0:1])
    ranked = (r1 < float(PEER_TOPK)).astype(F32) + (r2 < float(PEER_TOPK)).astype(F32)
    total = jnp.sum(ranked, axis=0, keepdims=True) + jnp.sum(chosen, axis=0, keepdims=True)
    clean = jnp.min(jnp.where(total == 3.0 * PEER_TOPK, 1.0, 0.0)) > 0.5
    return r2, e2, n1.astype(F32), e1, clean


def _twin_bf16_words(x):
    hi = pltpu.bitcast(x.astype(BF16).astype(F32), jnp.uint32)
    return hi | (hi >> 16)


def _plan_kernel(h2_ref, wq_ref, keys_ref, c2_ref, e2_ref, n1_ref, e1_ref, q_ref):
    q_ref[...] = jnp.dot(h2_ref[...], wq_ref[...], preferred_element_type=F32).astype(BF16)

    def per_head(h, carry):
        def scores(side):
            col = pl.multiple_of((2 * h + side) * PEER_HALF, PEER_HALF)
            qh = q_ref[:, pl.ds(col, PEER_HALF)]
            return lax.dot_general(keys_ref[2 * h + side], qh, (((1,), (1,)), ((), ())), preferred_element_type=F32)

        s1 = scores(0)
        s2 = scores(1)

        def store(r2, e2, n1, e1):
            c2_ref[h] = r2.astype(BF16)
            e2_ref[h] = e2.astype(BF16)
            for c in range(PLAN_TT // LANES):
                cols = slice(c * LANES, (c + 1) * LANES)
                n1_ref[h, c] = _twin_bf16_words(n1[:, cols])
                e1_ref[h, c] = _twin_bf16_words(e1[:, cols])

        r2, e2, n1, e1, clean = _plan_head(s1, s2, break_ties=False)
        store(r2, e2, n1, e1)

        @pl.when(jnp.logical_not(clean))
        def _():
            store(*_plan_head(s1, s2, break_ties=True)[:4])

        return carry

    lax.fori_loop(0, PEER_HEADS, per_head, 0)


def _plan(h2, wq, keys):
    blk = lambda i: (0, 0, i)
    shp = (PEER_HEADS, PEER_KEYS, TOKENS)
    lane_blk = lambda i: (0, i, 0, 0)
    lane_shp = (PEER_HEADS, TOKENS // LANES, PEER_KEYS, LANES)
    lane_spec = pl.BlockSpec((PEER_HEADS, PLAN_TT // LANES, PEER_KEYS, LANES), lane_blk)
    return pl.pallas_call(
        _plan_kernel,
        grid=(TOKENS // PLAN_TT,),
        in_specs=[
            pl.BlockSpec((PLAN_TT, D_MODEL), lambda i: (i, 0)),
            pl.BlockSpec(wq.shape, lambda i: (0, 0)),
            pl.BlockSpec(keys.shape, lambda i: (0, 0, 0)),
        ],
        out_specs=[pl.BlockSpec((PEER_HEADS, PEER_KEYS, PLAN_TT), blk)] * 2 + [lane_spec] * 2,
        out_shape=[
            jax.ShapeDtypeStruct(shp, BF16),
            jax.ShapeDtypeStruct(shp, BF16),
            jax.ShapeDtypeStruct(lane_shp, jnp.uint32),
            jax.ShapeDtypeStruct(lane_shp, jnp.uint32),
        ],
        scratch_shapes=[pltpu.VMEM((PLAN_TT, 2 * PEER_HEADS * PEER_HALF), BF16)],
        compiler_params=_cparams(("arbitrary",)),
        name="peer_plan",
    )(h2, wq, keys)


def _gelu_bf16(x):
    half_x = (0.5 * x).astype(BF16)
    return half_x + half_x * lax.erf(x * (1.0 / math.sqrt(2.0))).astype(BF16)


DENSE_NE = PEER_EXPERTS // DENSE_TE
DENSE_NT = TOKENS // DENSE_TT
DENSE_TILES = DENSE_NT * DENSE_NE
DENSE_KEYS_PER_TILE = DENSE_TE // PEER_KEYS


def _dense_stage_body(f, h2_ref, down_ref, upT_ref, c2_ref, e2_ref, n1_ref, e1_ref, acc_ref, act_w, act_r):
    gate_tile = jnp.clip(f - 1, 0, DENSE_TILES - 1) % DENSE_NE

    def act_piece(q, c):
        rows = slice(q * DENSE_SUB, (q + 1) * DENSE_SUB)
        cols = slice(c * DENSE_CHUNK, (c + 1) * DENSE_CHUNK)
        act_w[rows, cols] = jnp.dot(down_ref[rows, :], h2_ref[:, cols], preferred_element_type=F32)

    def gate_piece(k, c):
        i1 = gate_tile * DENSE_KEYS_PER_TILE + k
        cols = slice(c * DENSE_CHUNK, (c + 1) * DENSE_CHUNK)
        rows = slice(k * PEER_KEYS, (k + 1) * PEER_KEYS)

        def row_tile(ref, h):
            tiles = []
            for t in range(DENSE_CHUNK // LANES):
                lane_tile = c * (DENSE_CHUNK // LANES) + t
                words = jnp.broadcast_to(ref[h, lane_tile, pl.ds(i1, 1), :], (SUBLANES, LANES))
                tiles.append(jnp.concatenate([pltpu.bitcast(words, BF16)] * (PEER_KEYS // BF16_ROWS), axis=0))
            return jnp.concatenate(tiles, axis=1)

        gates = jnp.zeros((PEER_KEYS, DENSE_CHUNK), BF16)
        for h in range(PEER_HEADS):
            hit = c2_ref[h, :, cols] < row_tile(n1_ref, h)
            gates = gates + jnp.where(hit, e2_ref[h, :, cols], jnp.zeros((), BF16)) * row_tile(e1_ref, h)
        return gates * _gelu_bf16(act_r[rows, cols])

    keys_per_sub = DENSE_SUB // PEER_KEYS
    n_sub = DENSE_TE // DENSE_SUB
    for c in range(DENSE_TT // DENSE_CHUNK):
        cols = slice(c * DENSE_CHUNK, (c + 1) * DENSE_CHUNK)
        part = None
        for j in range(n_sub):
            w = jnp.concatenate([gate_piece(keys_per_sub * j + k, c) for k in range(keys_per_sub)], axis=0)
            d = jnp.dot(upT_ref[:, j * DENSE_SUB:(j + 1) * DENSE_SUB], w, preferred_element_type=F32)
            part = d if part is None else part + d
            act_piece(j, c)
        acc_ref[:, cols] += part


def _dense_kernel(h2_ref, down_ref, upT_ref, c2_ref, e2_ref, n1_ref, e1_ref, x1_ref, gate_ref, g_ref,
                  o_ref, acc_ref, act0_ref, act1_ref):
    f = pl.program_id(0)
    out_tile = jnp.clip(f - 1, 0, DENSE_TILES - 1) % DENSE_NE

    @pl.when(f == 0)
    def _():
        act1_ref[...] = jnp.zeros(act1_ref.shape, F32)

    @pl.when(out_tile == 0)
    def _():
        acc_ref[...] = jnp.zeros(acc_ref.shape, F32)

    stage = functools.partial(_dense_stage_body, f, h2_ref, down_ref, upT_ref, c2_ref, e2_ref, n1_ref, e1_ref,
                              acc_ref)

    @pl.when(f % 2 == 0)
    def _():
        stage(act0_ref, act1_ref)

    @pl.when(f % 2 == 1)
    def _():
        stage(act1_ref, act0_ref)

    @pl.when(jnp.logical_and(out_tile == DENSE_NE - 1, f >= 1))
    def _():
        x2 = x1_ref[...] + gate_ref[0] * acc_ref[...].T
        ms = jnp.mean(x2 * x2, axis=-1, keepdims=True)
        o_ref[...] = x2 * lax.rsqrt(ms + EPS) * g_ref[...]


def _dense(h2T, down, upT, c2, e2, n1, e1, x1, gate, final_g):
    tiles_per_batch = SEQ // DENSE_TT

    def tile(f, lag):
        return jnp.clip(f - lag, 0, DENSE_TILES - 1)

    plan = lambda f: (0, tile(f, 1) // DENSE_NE, 0, 0)
    out_tok = lambda f: (tile(f, 1) // DENSE_NE, 0)
    plan_blk = (PEER_HEADS, DENSE_TT // LANES, PEER_KEYS, LANES)
    rank_map = lambda f: (0, 0, tile(f, 1) // DENSE_NE)
    rank_blk = (PEER_HEADS, PEER_KEYS, DENSE_TT)
    return pl.pallas_call(
        _dense_kernel,
        grid=(DENSE_TILES + 1,),
        in_specs=[
            pl.BlockSpec((D_MODEL, DENSE_TT), lambda f: (0, tile(f, 0) // DENSE_NE)),
            pl.BlockSpec((DENSE_TE, D_MODEL), lambda f: (tile(f, 0) % DENSE_NE, 0)),
            pl.BlockSpec((D_MODEL, DENSE_TE), lambda f: (0, tile(f, 1) % DENSE_NE)),
            pl.BlockSpec(rank_blk, rank_map),
            pl.BlockSpec(rank_blk, rank_map),
            pl.BlockSpec(plan_blk, plan),
            pl.BlockSpec(plan_blk, plan),
            pl.BlockSpec((DENSE_TT, D_MODEL), out_tok),
            pl.BlockSpec((1, 1, D_MODEL), lambda f: (tile(f, 1) // DENSE_NE // tiles_per_batch, 0, 0)),
            pl.BlockSpec((1, D_MODEL), lambda f: (0, 0)),
        ],
        out_specs=pl.BlockSpec((DENSE_TT, D_MODEL), out_tok),
        out_shape=jax.ShapeDtypeStruct((TOKENS, D_MODEL), F32),
        scratch_shapes=[
            pltpu.VMEM((D_MODEL, DENSE_TT), F32),
            pltpu.VMEM((DENSE_TE, DENSE_TT), F32),
            pltpu.VMEM((DENSE_TE, DENSE_TT), F32),
        ],
        compiler_params=_cparams(("arbitrary",)),
        name="peer_dense",
    )(h2T, down, upT, c2, e2, n1, e1, x1, gate, final_g)


def _transpose_kernel(x_ref, o_ref):
    o_ref[...] = x_ref[...].T.astype(BF16)


def _transpose_table(t):
    rows, cols = t.shape
    tr = 512
    return pl.pallas_call(
        _transpose_kernel,
        grid=(rows // tr,),
        in_specs=[pl.BlockSpec((tr, cols), lambda i: (i, 0))],
        out_specs=pl.BlockSpec((cols, tr), lambda i: (0, i)),
        out_shape=jax.ShapeDtypeStruct((cols, rows), BF16),
        compiler_params=_cparams(("arbitrary",)),
        name="table_transpose",
    )(t)


def _position_dft_tables():
    radix = 64
    r = lax.broadcasted_iota(jnp.int32, (radix, SEQ), 0)
    k = lax.broadcasted_iota(jnp.int32, (radix, SEQ), 1)
    coarse = ((r * k) % radix).astype(F32) * (2.0 * math.pi / radix)
    fine = ((r * k) % SEQ).astype(F32) * (2.0 * math.pi / SEQ)
    scale = 1.0 / math.sqrt(SEQ)
    ca, sa = jnp.cos(coarse)[:, None, :] * scale, jnp.sin(coarse)[:, None, :] * scale
    cb, sb = jnp.cos(fine)[None, :, :], jnp.sin(fine)[None, :, :]
    cos_tab = (ca * cb - sa * sb).reshape(SEQ, SEQ).astype(BF16)
    sin_tab = (sa * cb + ca * sb).reshape(SEQ, SEQ).astype(BF16)
    return cos_tab, sin_tab


def _channel_dft_table():
    j = lax.broadcasted_iota(jnp.int32, (FNET_GROUP, FNET_GROUP), 0)
    k = lax.broadcasted_iota(jnp.int32, (FNET_GROUP, FNET_GROUP), 1)
    ang = ((j * k) % FNET_GROUP).astype(F32) * (2.0 * math.pi / FNET_GROUP)
    scale = 1.0 / math.sqrt(FNET_GROUP)
    return jnp.concatenate([jnp.cos(ang) * scale, jnp.sin(ang) * scale], axis=1).astype(BF16)


def _pad_lanes(v, width):
    return jnp.pad(v, ((0, 0), (0, width - v.shape[1])))


def kernel(x, c, w_ada, b_ada, norm_mix_g, w_in, conv_w, conv_b, a_log_fwd, a_log_bwd, dt_bias_fwd, dt_bias_bwd, d_skip, ssm_norm_g, w_out, norm_ffn_g, w_query, sub_keys, expert_down, expert_up, final_norm_g):
    assert w_ada.shape[0] == 1, "single-layer problem: the final RMSNorm is fused into the PEER kernel"
    xt = x.reshape(TOKENS, D_MODEL)
    c_pad = jnp.pad(c, ((0, SUBLANES - BATCH), (0, 0)))
    cs_tab, ss_tab = _position_dft_tables()
    csc_tab = _channel_dft_table()
    for layer in range(1):
        mod = _adaln(c_pad, w_ada[layer], b_ada[layer][None, :])[:BATCH]
        shift_m, scale_m, gate_m, shift_f, scale_f, gate_f = [
            m.reshape(BATCH, 1, D_MODEL) for m in jnp.split(mod, 6, axis=-1)]

        w = w_in[layer]
        wf = w[:, :D_FNET].astype(BF16)
        wz = w[:, D_FNET:D_FNET + D_SSM].astype(BF16)
        wx = w[:, D_FNET + D_SSM:D_FNET + D_SSM + D_CONV].astype(BF16)
        wdt = _pad_lanes(w[:, D_FNET + D_SSM + D_CONV:], DT_PAD).astype(BF16)
        uc, us, z, xbc, dt, dtT = _inproj(xt, norm_mix_g[layer][None, :], shift_m, scale_m,
                                          wf, wz, wx, wdt, wdt.T, csc_tab)
        y_fnet = _dft(cs_tab, ss_tab, uc, us)

        xconv = _conv(xbc.reshape(BATCH, SEQ, D_CONV), conv_w[layer], conv_b[layer][None, :])
        alog_row = _pad_lanes(jnp.concatenate([a_log_fwd[layer], a_log_bwd[layer]])[None, :], DT_PAD)
        bias_row = _pad_lanes(jnp.concatenate([dt_bias_fwd[layer], dt_bias_bwd[layer]])[None, :], DT_PAD)
        dskip_row = jnp.repeat(d_skip[layer], SSM_HEAD_DIM)[None, :]
        y_ssm = _ssd(xconv.reshape(TOKENS, D_CONV), dt, dtT, z, alog_row, bias_row, alog_row.T, bias_row.T,
                     dskip_row, ssm_norm_g[layer][None, :])

        wo = w_out[layer].astype(BF16)
        x1, h2, h2T = _outproj(xt, y_fnet, y_ssm, wo[:D_FNET], wo[D_FNET:], gate_m,
                          norm_ffn_g[layer][None, :], shift_f, scale_f)

        keys = sub_keys[layer].reshape(2 * PEER_HEADS, PEER_KEYS, PEER_HALF).astype(BF16)
        c2, e2, n1, e1 = _plan(h2, w_query[layer].astype(BF16), keys)
        xt = _dense(h2T, expert_down[layer].astype(BF16), _transpose_table(expert_up[layer]),
                    c2, e2, n1, e1, x1, gate_f, final_norm_g[None, :])
    return xt.reshape(BATCH, SEQ, D_MODEL)
```

```python
import functools
import math

import jax
import jax.numpy as jnp
from jax import lax
from jax.experimental import pallas as pl
from jax.experimental.pallas import tpu as pltpu

F32 = jnp.float32
BF16 = jnp.bfloat16
HIGHEST = lax.Precision.HIGHEST

D_MODEL = 1024
BATCH = 4
SEQ = 4096
TOKENS = BATCH * SEQ
D_MIX = 2 * D_MODEL
D_FNET = D_MIX // 4
N_FNET_GROUPS = 4
FNET_GROUP = D_FNET // N_FNET_GROUPS
D_SSM = D_MIX - D_FNET
SSM_HEAD_DIM = 64
N_SSM_HEADS = D_SSM // SSM_HEAD_DIM
N_BC_GROUPS = 4
HEADS_PER_GROUP = N_SSM_HEADS // N_BC_GROUPS
D_STATE = 128
CONV_WIDTH = 5
SSD_CHUNK = 128
N_CHUNKS = SEQ // SSD_CHUNK
D_BC = N_BC_GROUPS * D_STATE
D_CONV = D_SSM + 2 * D_BC
GROUP_WIDTH = D_SSM // N_BC_GROUPS
PEER_HEADS = 8
PEER_KEYS = 128
PEER_EXPERTS = PEER_KEYS * PEER_KEYS
PEER_HALF = 128
PEER_TOPK = 16
EPS = 1e-6

LANES = 128
SUBLANES = 8
BF16_ROWS = 16
DT_PAD = LANES
VMEM_LIMIT = 56 * 1024 * 1024

ADA_TN = 512
PROJ_TM = 512
DFT_TM = 512
CONV_TC = 256
CONV_TR = 512
CONV_HALO = SUBLANES
OUT_TM = 512
PLAN_TT = 256
DENSE_TT = 512
DENSE_TE = 2048
DENSE_SUB = 256
DENSE_CHUNK = 256


def _cparams(sem):
    return pltpu.CompilerParams(dimension_semantics=sem, vmem_limit_bytes=VMEM_LIMIT)


def _sigmoid(x):
    return 1.0 / (1.0 + jnp.exp(-x))


def _bf16_terms(x):
    terms = []
    rest = x
    for _ in range(3):
        term = rest.astype(BF16)
        terms.append(term)
        rest = rest - term.astype(F32)
    return terms


def _softplus(x):
    return jnp.maximum(x, 0.0) + jnp.log(1.0 + jnp.exp(-jnp.abs(x)))


def _adaln_kernel(c_ref, w_ref, b_ref, o_ref):
    c = c_ref[...]
    ca = c * _sigmoid(c)
    o_ref[...] = jnp.dot(ca, w_ref[...], preferred_element_type=F32, precision=HIGHEST) + b_ref[...]


def _adaln(c_pad, w_ada, b_ada):
    n = w_ada.shape[1]
    return pl.pallas_call(
        _adaln_kernel,
        grid=(n // ADA_TN,),
        in_specs=[
            pl.BlockSpec((SUBLANES, D_MODEL), lambda j: (0, 0)),
            pl.BlockSpec((D_MODEL, ADA_TN), lambda j: (0, j)),
            pl.BlockSpec((1, ADA_TN), lambda j: (0, j)),
        ],
        out_specs=pl.BlockSpec((SUBLANES, ADA_TN), lambda j: (0, j)),
        out_shape=jax.ShapeDtypeStruct((SUBLANES, n), F32),
        compiler_params=_cparams(("arbitrary",)),
        name="adaln",
    )(c_pad, w_ada, b_ada)


def _modulated_norm(x, gain, shift, scale):
    ms = jnp.mean(x * x, axis=-1, keepdims=True)
    xn = x * lax.rsqrt(ms + EPS) * gain
    return xn * (1.0 + scale) + shift


def _inproj_kernel(x_ref, g_ref, shift_ref, scale_ref, wf_ref, wz_ref, wx_ref, wdt_ref, wdtT_ref, csc_ref,
                   uc_ref, us_ref, z_ref, xbc_ref, dt_ref, dtT_ref):
    h = _modulated_norm(x_ref[...], g_ref[...], shift_ref[0], scale_ref[0])
    hb = h.astype(BF16)
    fb = jnp.dot(hb, wf_ref[...], preferred_element_type=F32).astype(BF16)
    for g in range(N_FNET_GROUPS):
        sl = slice(g * FNET_GROUP, (g + 1) * FNET_GROUP)
        u = jnp.dot(fb[:, sl], csc_ref[...], preferred_element_type=F32)
        uc_ref[:, sl] = u[:, :FNET_GROUP].astype(BF16)
        us_ref[:, sl] = u[:, FNET_GROUP:].astype(BF16)
    z_ref[...] = jnp.dot(hb, wz_ref[...], preferred_element_type=F32).astype(BF16)
    xbc_ref[...] = jnp.dot(hb, wx_ref[...], preferred_element_type=F32).astype(BF16)
    dt_ref[...] = jnp.dot(hb, wdt_ref[...], preferred_element_type=F32)
    dtT_ref[...] = lax.dot_general(wdtT_ref[...], hb, (((1,), (1,)), ((), ())), preferred_element_type=F32)


def _inproj(x2d, gain, shift, scale, wf, wz, wx, wdt, wdtT, csc):
    tiles_per_batch = SEQ // PROJ_TM
    row = lambda i: (i, 0)
    const = lambda i: (0, 0)
    per_batch = lambda i: (i // tiles_per_batch, 0, 0)
    return pl.pallas_call(
        _inproj_kernel,
        grid=(TOKENS // PROJ_TM,),
        in_specs=[
            pl.BlockSpec((PROJ_TM, D_MODEL), row),
            pl.BlockSpec((1, D_MODEL), const),
            pl.BlockSpec((1, 1, D_MODEL), per_batch),
            pl.BlockSpec((1, 1, D_MODEL), per_batch),
            pl.BlockSpec(wf.shape, const),
            pl.BlockSpec(wz.shape, const),
            pl.BlockSpec(wx.shape, const),
            pl.BlockSpec(wdt.shape, const),
            pl.BlockSpec(wdtT.shape, const),
            pl.BlockSpec(csc.shape, const),
        ],
        out_specs=[
            pl.BlockSpec((PROJ_TM, D_FNET), row),
            pl.BlockSpec((PROJ_TM, D_FNET), row),
            pl.BlockSpec((PROJ_TM, D_SSM), row),
            pl.BlockSpec((PROJ_TM, D_CONV), row),
            pl.BlockSpec((PROJ_TM, DT_PAD), row),
            pl.BlockSpec((DT_PAD, PROJ_TM), lambda i: (0, i)),
        ],
        out_shape=[
            jax.ShapeDtypeStruct((TOKENS, D_FNET), BF16),
            jax.ShapeDtypeStruct((TOKENS, D_FNET), BF16),
            jax.ShapeDtypeStruct((TOKENS, D_SSM), BF16),
            jax.ShapeDtypeStruct((TOKENS, D_CONV), BF16),
            jax.ShapeDtypeStruct((TOKENS, DT_PAD), F32),
            jax.ShapeDtypeStruct((DT_PAD, TOKENS), F32),
        ],
        compiler_params=_cparams(("arbitrary",)),
        name="inproj",
    )(x2d, gain, shift, scale, wf, wz, wx, wdt, wdtT, csc)


def _dft_kernel(cs_ref, ss_ref, uc_ref, us_ref, o_ref):
    y = jnp.dot(cs_ref[...], uc_ref[...], preferred_element_type=F32)
    y = y - jnp.dot(ss_ref[...], us_ref[...], preferred_element_type=F32)
    o_ref[...] = y.astype(BF16)


def _dft(cs, ss, uc, us):
    mt = SEQ // DFT_TM
    return pl.pallas_call(
        _dft_kernel,
        grid=(BATCH, mt),
        in_specs=[
            pl.BlockSpec((DFT_TM, SEQ), lambda b, m: (m, 0)),
            pl.BlockSpec((DFT_TM, SEQ), lambda b, m: (m, 0)),
            pl.BlockSpec((SEQ, D_FNET), lambda b, m: (b, 0)),
            pl.BlockSpec((SEQ, D_FNET), lambda b, m: (b, 0)),
        ],
        out_specs=pl.BlockSpec((DFT_TM, D_FNET), lambda b, m: (b * mt + m, 0)),
        out_shape=jax.ShapeDtypeStruct((TOKENS, D_FNET), BF16),
        compiler_params=_cparams(("arbitrary", "arbitrary")),
        name="fnet_dft",
    )(cs, ss, uc, us)


def _conv_kernel(x_ref, w_ref, b_ref, o_ref, pad_ref):
    zeros = jnp.zeros((CONV_HALO, CONV_TC), F32)
    pad_ref[0:CONV_HALO, :] = zeros
    pad_ref[SEQ + CONV_HALO:SEQ + 2 * CONV_HALO, :] = zeros
    pad_ref[CONV_HALO:SEQ + CONV_HALO, :] = x_ref[0].astype(F32)
    w = w_ref[...]
    bias = b_ref[...]
    first = CONV_HALO - CONV_WIDTH // 2
    for i in range(SEQ // CONV_TR):
        acc = jnp.broadcast_to(bias, (CONV_TR, CONV_TC))
        for k in range(CONV_WIDTH):
            lo = first + k + i * CONV_TR
            acc = acc + pad_ref[lo:lo + CONV_TR, :] * w[k:k + 1, :]
        o_ref[0, i * CONV_TR:(i + 1) * CONV_TR, :] = (acc * _sigmoid(acc)).astype(BF16)


def _conv(xbc3, conv_w, conv_b):
    return pl.pallas_call(
        _conv_kernel,
        grid=(BATCH, D_CONV // CONV_TC),
        in_specs=[
            pl.BlockSpec((1, SEQ, CONV_TC), lambda b, j: (b, 0, j)),
            pl.BlockSpec((CONV_WIDTH, CONV_TC), lambda b, j: (0, j)),
            pl.BlockSpec((1, CONV_TC), lambda b, j: (0, j)),
        ],
        out_specs=pl.BlockSpec((1, SEQ, CONV_TC), lambda b, j: (b, 0, j)),
        out_shape=jax.ShapeDtypeStruct((BATCH, SEQ, D_CONV), BF16),
        scratch_shapes=[pltpu.VMEM((SEQ + 2 * CONV_HALO, CONV_TC), F32)],
        compiler_params=_cparams(("arbitrary", "arbitrary")),
        name="conv_silu",
    )(xbc3, conv_w, conv_b)


def _ssd_direction(xc, dtvT, cs, csT, a, aT, decay_row, state_ref, yacc_ref, head_base, forward):
    li = lax.broadcasted_iota(jnp.int32, (SSD_CHUNK, SSD_CHUNK), 0)
    si = lax.broadcasted_iota(jnp.int32, (SSD_CHUNK, SSD_CHUNK), 1)
    low_half = si < SSM_HEAD_DIM
    if forward:
        u, uT = cs, csT
        mask = li >= si
    else:
        u, uT = cs - a, csT - aT
        mask = si >= li
    pair = 2 * SSM_HEAD_DIM
    for g in range(N_BC_GROUPS):
        bc = xc[:, D_SSM + g * D_STATE:D_SSM + (g + 1) * D_STATE]
        cc = xc[:, D_SSM + D_BC + g * D_STATE:D_SSM + D_BC + (g + 1) * D_STATE]
        gmat = lax.dot_general(cc, bc, (((1,), (1,)), ((), ())), preferred_element_type=F32)
        bcT = bc.astype(F32).T
        ccf = cc.astype(F32)
        for pr in range(HEADS_PER_GROUP // 2):
            h0 = g * HEADS_PER_GROUP + 2 * pr
            xcols = slice(h0 * SSM_HEAD_DIM, h0 * SSM_HEAD_DIM + pair)
            scols = slice(pr * pair, (pr + 1) * pair)
            xpair = xc[:, xcols]
            spair = state_ref[g, :, scols]
            rhs = jnp.concatenate([xpair, spair.astype(BF16)], axis=0)
            ys, upds = [], []
            for j in range(2):
                hh = head_base + h0 + j
                ub = jnp.broadcast_to(u[:, hh:hh + 1], (SSD_CHUNK, SSD_CHUNK))
                urow = uT[hh:hh + 1, :]
                dtrow = dtvT[hh:hh + 1, :]
                tot = cs[SSD_CHUNK - 1:SSD_CHUNK, hh:hh + 1]
                if forward:
                    diff = ub - urow
                    cscale = jnp.exp(ub)
                    wrow = jnp.exp(tot - urow) * dtrow
                else:
                    diff = urow - ub
                    cscale = jnp.exp(tot - ub)
                    wrow = jnp.exp(urow) * dtrow
                lmat = jnp.where(mask, jnp.exp(diff), 0.0)
                lhs = jnp.concatenate([(gmat * lmat * dtrow).astype(BF16), (ccf * cscale).astype(BF16)], axis=1)
                ys.append(jnp.dot(lhs, rhs, preferred_element_type=F32))
                upds.append(jnp.dot((bcT * wrow).astype(BF16), xpair, preferred_element_type=F32))
            yacc_ref[:, xcols] = jnp.where(low_half, ys[0], ys[1])
            state_ref[g, :, scols] = spair * decay_row[:, xcols] + jnp.where(low_half, upds[0], upds[1])


def _ssd_kernel(xc_ref, dt_ref, dtT_ref, z_ref, alog_row_ref, bias_row_ref, alog_col_ref, bias_col_ref,
                dskip_ref, ng_ref, expand_ref, y_ref, state_ref, yacc_ref, yb_ref):
    phase = pl.program_id(1)
    step = pl.program_id(2)

    @pl.when(step == 0)
    def _():
        state_ref[...] = jnp.zeros(state_ref.shape, F32)

    xc = xc_ref[...]
    dtv = _softplus(dt_ref[...] + bias_row_ref[...])
    a = dtv * (-jnp.exp(alog_row_ref[...]))
    used = 2 * N_SSM_HEADS
    dtvT = _softplus(dtT_ref[0:used, :] + bias_col_ref[0:used, :])
    aT = dtvT * (-jnp.exp(alog_col_ref[0:used, :]))
    li = lax.broadcasted_iota(jnp.int32, (SSD_CHUNK, SSD_CHUNK), 0)
    si = lax.broadcasted_iota(jnp.int32, (SSD_CHUNK, SSD_CHUNK), 1)
    lower = (li >= si).astype(BF16)
    upper = (li <= si).astype(BF16)
    cs = sum(jnp.dot(lower, t, preferred_element_type=F32) for t in _bf16_terms(a))
    csT = sum(jnp.dot(t, upper, preferred_element_type=F32) for t in _bf16_terms(aT))
    chunk_decay = jnp.exp(cs[SSD_CHUNK - 1:SSD_CHUNK, :])

    def decay_row(direction):
        return sum(jnp.dot(t, expand_ref[direction], preferred_element_type=F32) for t in _bf16_terms(chunk_decay))

    @pl.when(phase == 0)
    def _():
        chunk = N_CHUNKS - 1 - step
        _ssd_direction(xc, dtvT, cs, csT, a, aT, decay_row(1), state_ref, yacc_ref, N_SSM_HEADS, False)
        yb_ref[chunk] = yacc_ref[...].astype(BF16)

    @pl.when(phase == 1)
    def _():
        _ssd_direction(xc, dtvT, cs, csT, a, aT, decay_row(0), state_ref, yacc_ref, 0, True)
        xs = xc[:, :D_SSM].astype(F32)
        y = yacc_ref[...] + yb_ref[step].astype(F32) + xs * dskip_ref[...]
        z = z_ref[...].astype(F32)
        y = y * (z * _sigmoid(z))
        for g in range(N_BC_GROUPS):
            gs = slice(g * GROUP_WIDTH, (g + 1) * GROUP_WIDTH)
            yg = y[:, gs]
            ms = jnp.mean(yg * yg, axis=-1, keepdims=True)
            y_ref[:, gs] = (yg * lax.rsqrt(ms + EPS) * ng_ref[:, gs]).astype(BF16)


def _head_expansion():
    col = lax.broadcasted_iota(jnp.int32, (2, DT_PAD, D_SSM), 1)
    head = lax.broadcasted_iota(jnp.int32, (2, DT_PAD, D_SSM), 2) // SSM_HEAD_DIM
    direction = lax.broadcasted_iota(jnp.int32, (2, DT_PAD, D_SSM), 0)
    return (col == direction * N_SSM_HEADS + head).astype(BF16)


def _ssd(xconv, dt, dtT, z, alog_row, bias_row, alog_col, bias_col, dskip_row, norm_g):
    def chunk_of(p, s):
        return p * s + (1 - p) * (N_CHUNKS - 1 - s)

    rows = lambda b, p, s: (b * N_CHUNKS + chunk_of(p, s), 0)
    cols = lambda b, p, s: (0, b * N_CHUNKS + chunk_of(p, s))
    fwd_only = lambda b, p, s: (b * N_CHUNKS + p * s, 0)
    const = lambda b, p, s: (0, 0)
    return pl.pallas_call(
        _ssd_kernel,
        grid=(BATCH, 2, N_CHUNKS),
        in_specs=[
            pl.BlockSpec((SSD_CHUNK, D_CONV), rows),
            pl.BlockSpec((SSD_CHUNK, DT_PAD), rows),
            pl.BlockSpec((DT_PAD, SSD_CHUNK), cols),
            pl.BlockSpec((SSD_CHUNK, D_SSM), fwd_only),
            pl.BlockSpec((1, DT_PAD), const),
            pl.BlockSpec((1, DT_PAD), const),
            pl.BlockSpec((DT_PAD, 1), const),
            pl.BlockSpec((DT_PAD, 1), const),
            pl.BlockSpec((1, D_SSM), const),
            pl.BlockSpec((1, D_SSM), const),
            pl.BlockSpec((2, DT_PAD, D_SSM), lambda b, p, s: (0, 0, 0)),
        ],
        out_specs=pl.BlockSpec((SSD_CHUNK, D_SSM), fwd_only),
        out_shape=jax.ShapeDtypeStruct((TOKENS, D_SSM), BF16),
        scratch_shapes=[
            pltpu.VMEM((N_BC_GROUPS, D_STATE, GROUP_WIDTH), F32),
            pltpu.VMEM((SSD_CHUNK, D_SSM), F32),
            pltpu.VMEM((N_CHUNKS, SSD_CHUNK, D_SSM), BF16),
        ],
        compiler_params=_cparams(("arbitrary", "arbitrary", "arbitrary")),
        name="ssd_scan",
    )(xconv, dt, dtT, z, alog_row, bias_row, alog_col, bias_col, dskip_row, norm_g, _head_expansion())


def _outproj_kernel(x_ref, yf_ref, ys_ref, wf_ref, ws_ref, gate_ref, g_ref, shift_ref, scale_ref,
                    x1_ref, h2_ref, h2T_ref):
    mix = jnp.dot(yf_ref[...], wf_ref[...], preferred_element_type=F32)
    mix = mix + jnp.dot(ys_ref[...], ws_ref[...], preferred_element_type=F32)
    x1 = x_ref[...] + gate_ref[0] * mix
    x1_ref[...] = x1
    h2 = _modulated_norm(x1, g_ref[...], shift_ref[0], scale_ref[0])
    h2_ref[...] = h2.astype(BF16)
    h2T_ref[...] = h2.T.astype(BF16)


def _outproj(x2d, yf, ys, wf, ws, gate, gain, shift, scale):
    tiles_per_batch = SEQ // OUT_TM
    row = lambda i: (i, 0)
    const = lambda i: (0, 0)
    per_batch = lambda i: (i // tiles_per_batch, 0, 0)
    return pl.pallas_call(
        _outproj_kernel,
        grid=(TOKENS // OUT_TM,),
        in_specs=[
            pl.BlockSpec((OUT_TM, D_MODEL), row),
            pl.BlockSpec((OUT_TM, D_FNET), row),
            pl.BlockSpec((OUT_TM, D_SSM), row),
            pl.BlockSpec(wf.shape, const),
            pl.BlockSpec(ws.shape, const),
            pl.BlockSpec((1, 1, D_MODEL), per_batch),
            pl.BlockSpec((1, D_MODEL), const),
            pl.BlockSpec((1, 1, D_MODEL), per_batch),
            pl.BlockSpec((1, 1, D_MODEL), per_batch),
        ],
        out_specs=[pl.BlockSpec((OUT_TM, D_MODEL), row), pl.BlockSpec((OUT_TM, D_MODEL), row),
                   pl.BlockSpec((D_MODEL, OUT_TM), lambda i: (0, i))],
        out_shape=[jax.ShapeDtypeStruct((TOKENS, D_MODEL), F32), jax.ShapeDtypeStruct((TOKENS, D_MODEL), BF16),
                   jax.ShapeDtypeStruct((D_MODEL, TOKENS), BF16)],
        compiler_params=_cparams(("arbitrary",)),
        name="outproj",
    )(x2d, yf, ys, wf, ws, gate, gain, shift, scale)


MARKER_UNIT = 2.0 ** 100


def _extract_topk_marked(x, k):
    work = x
    vals = []
    for r in range(k):
        m = jnp.max(work, axis=0, keepdims=True)
        work = jnp.where(work == m, -(k + r) * MARKER_UNIT, work)
        vals.append(m)
    marked = work <= -k * MARKER_UNIT
    rank = jnp.where(marked, work * (-1.0 / MARKER_UNIT) - float(k), float(k))
    in_range = jnp.min(x, axis=0, keepdims=True) > -MARKER_UNIT
    return jnp.concatenate(vals, axis=0), rank, in_range


def _extract_topk(x, k, break_ties):
    rows = x.shape[0]
    iota = lax.broadcasted_iota(jnp.int32, x.shape, 0).astype(F32)
    rank = jnp.full(x.shape, float(k), F32)
    work = x
    vals = []
    for r in range(k):
        m = jnp.max(work, axis=0, keepdims=True)
        sel = work == m
        if break_ties:
            first = jnp.min(jnp.where(sel, iota, float(rows)), axis=0, keepdims=True)
            sel = iota == first
        rank = jnp.where(sel, float(r), rank)
        work = jnp.where(sel, -jnp.inf, work)
        vals.append(m)
    return jnp.concatenate(vals, axis=0), rank


def _candidate_width(r):
    return PEER_TOPK // (r + 1)


def _candidates(v1, v2):
    sub = lax.broadcasted_iota(jnp.int32, (SUBLANES, v1.shape[1]), 0)
    blocks = [v1[0:1] + v2, v1[1:2] + v2[0:SUBLANES]]
    for r in range(2, SUBLANES):
        blocks.append(jnp.where(sub < _candidate_width(r), v1[r:r + 1] + v2[0:SUBLANES], -jnp.inf))
    blocks.append(v1[SUBLANES:] + v2[0:1])
    return jnp.concatenate(blocks, axis=0)


def _row_counts(chosen):
    counts = [jnp.sum(chosen[0:PEER_TOPK], axis=0, keepdims=True)]
    for r in range(1, SUBLANES):
        lo = PEER_TOPK + (r - 1) * SUBLANES
        counts.append(jnp.sum(chosen[lo:lo + SUBLANES], axis=0, keepdims=True))
    tail = chosen[PEER_TOPK + (SUBLANES - 1) * SUBLANES:]
    counts.extend(tail[j:j + 1] for j in range(SUBLANES))
    return counts


def _plan_head(s1, s2, break_ties):
    if break_ties:
        v1, r1 = _extract_topk(s1, PEER_TOPK, True)
        v2, r2 = _extract_topk(s2, PEER_TOPK, True)
        in_range = None
    else:
        v1, r1, ok1 = _extract_topk_marked(s1, PEER_TOPK)
        v2, r2, ok2 = _extract_topk_marked(s2, PEER_TOPK)
        in_range = jnp.logical_and(ok1, ok2)
    cand = _candidates(v1, v2)
    _, crank = _extract_topk(cand, PEER_TOPK, break_ties)
    chosen = (crank < float(PEER_TOPK)).astype(F32)
    counts = _row_counts(chosen)
    top = v1[0:1] + v2[0:1]
    zsum = jnp.sum(chosen * jnp.exp(cand - top), axis=0, keepdims=True)
    r1b = r1.astype(BF16)
    n1 = jnp.zeros(s1.shape, BF16)
    for r in range(PEER_TOPK):
        n1 = jnp.where(r1b == float(r), counts[r].astype(BF16), n1)
    e1 = jnp.exp(s1 - v1[0:1]) / zsum
    e2 = jnp.exp(s2 - v2[0:1])
    ranked = (r1 < float(PEER_TOPK)).astype(F32) + (r2 < float(PEER_TOPK)).astype(F32)
    total = jnp.sum(ranked, axis=0, keepdims=True) + jnp.sum(chosen, axis=0, keepdims=True)
    good = total == 3.0 * PEER_TOPK
    if in_range is not None:
        good = jnp.logical_and(good, in_range)
    clean = jnp.min(jnp.where(good, 1.0, 0.0)) > 0.5
    return r2, e2, n1.astype(F32), e1, clean


def _twin_bf16_words(x):
    hi = pltpu.bitcast(x.astype(BF16).astype(F32), jnp.uint32)
    return hi | (hi >> 16)


def _plan_kernel(h2_ref, wq_ref, keys_ref, c2_ref, e2_ref, n1_ref, e1_ref, q_ref):
    q_ref[...] = jnp.dot(h2_ref[...], wq_ref[...], preferred_element_type=F32).astype(BF16)

    def per_head(h, carry):
        def scores(side):
            col = pl.multiple_of((2 * h + side) * PEER_HALF, PEER_HALF)
            qh = q_ref[:, pl.ds(col, PEER_HALF)]
            return lax.dot_general(keys_ref[2 * h + side], qh, (((1,), (1,)), ((), ())), preferred_element_type=F32)

        s1 = scores(0)
        s2 = scores(1)

        def store(r2, e2, n1, e1):
            c2_ref[h] = r2.astype(BF16)
            e2_ref[h] = e2.astype(BF16)
            for c in range(PLAN_TT // LANES):
                cols = slice(c * LANES, (c + 1) * LANES)
                n1_ref[h, c] = _twin_bf16_words(n1[:, cols])
                e1_ref[h, c] = _twin_bf16_words(e1[:, cols])

        r2, e2, n1, e1, clean = _plan_head(s1, s2, break_ties=False)
        store(r2, e2, n1, e1)

        @pl.when(jnp.logical_not(clean))
        def _():
            store(*_plan_head(s1, s2, break_ties=True)[:4])

        return carry

    lax.fori_loop(0, PEER_HEADS, per_head, 0)


def _plan(h2, wq, keys):
    blk = lambda i: (0, 0, i)
    shp = (PEER_HEADS, PEER_KEYS, TOKENS)
    lane_blk = lambda i: (0, i, 0, 0)
    lane_shp = (PEER_HEADS, TOKENS // LANES, PEER_KEYS, LANES)
    lane_spec = pl.BlockSpec((PEER_HEADS, PLAN_TT // LANES, PEER_KEYS, LANES), lane_blk)
    return pl.pallas_call(
        _plan_kernel,
        grid=(TOKENS // PLAN_TT,),
        in_specs=[
            pl.BlockSpec((PLAN_TT, D_MODEL), lambda i: (i, 0)),
            pl.BlockSpec(wq.shape, lambda i: (0, 0)),
            pl.BlockSpec(keys.shape, lambda i: (0, 0, 0)),
        ],
        out_specs=[pl.BlockSpec((PEER_HEADS, PEER_KEYS, PLAN_TT), blk)] * 2 + [lane_spec] * 2,
        out_shape=[
            jax.ShapeDtypeStruct(shp, BF16),
            jax.ShapeDtypeStruct(shp, BF16),
            jax.ShapeDtypeStruct(lane_shp, jnp.uint32),
            jax.ShapeDtypeStruct(lane_shp, jnp.uint32),
        ],
        scratch_shapes=[pltpu.VMEM((PLAN_TT, 2 * PEER_HEADS * PEER_HALF), BF16)],
        compiler_params=_cparams(("arbitrary",)),
        name="peer_plan",
    )(h2, wq, keys)


def _gelu_bf16(x):
    half_x = (0.5 * x).astype(BF16)
    return half_x + half_x * lax.erf(x * (1.0 / math.sqrt(2.0))).astype(BF16)


DENSE_NE = PEER_EXPERTS // DENSE_TE
DENSE_NT = TOKENS // DENSE_TT
DENSE_TILES = DENSE_NT * DENSE_NE
DENSE_KEYS_PER_TILE = DENSE_TE // PEER_KEYS


def _dense_stage_body(f, h2_ref, down_ref, upT_ref, c2_ref, e2_ref, n1_ref, e1_ref, acc_ref, act_w, act_r):
    gate_tile = jnp.clip(f - 1, 0, DENSE_TILES - 1) % DENSE_NE

    def act_piece(q, c):
        rows = slice(q * DENSE_SUB, (q + 1) * DENSE_SUB)
        cols = slice(c * DENSE_CHUNK, (c + 1) * DENSE_CHUNK)
        act_w[rows, cols] = jnp.dot(down_ref[rows, :], h2_ref[:, cols], preferred_element_type=F32)

    def gate_piece(k, c):
        i1 = gate_tile * DENSE_KEYS_PER_TILE + k
        cols = slice(c * DENSE_CHUNK, (c + 1) * DENSE_CHUNK)
        rows = slice(k * PEER_KEYS, (k + 1) * PEER_KEYS)

        def row_tile(ref, h):
            tiles = []
            for t in range(DENSE_CHUNK // LANES):
                lane_tile = c * (DENSE_CHUNK // LANES) + t
                words = jnp.broadcast_to(ref[h, lane_tile, pl.ds(i1, 1), :], (SUBLANES, LANES))
                tiles.append(jnp.concatenate([pltpu.bitcast(words, BF16)] * (PEER_KEYS // BF16_ROWS), axis=0))
            return jnp.concatenate(tiles, axis=1)

        gates = jnp.zeros((PEER_KEYS, DENSE_CHUNK), BF16)
        for h in range(PEER_HEADS):
            hit = c2_ref[h, :, cols] < row_tile(n1_ref, h)
            gates = gates + jnp.where(hit, e2_ref[h, :, cols], jnp.zeros((), BF16)) * row_tile(e1_ref, h)
        return gates * _gelu_bf16(act_r[rows, cols])

    keys_per_sub = DENSE_SUB // PEER_KEYS
    n_sub = DENSE_TE // DENSE_SUB
    for c in range(DENSE_TT // DENSE_CHUNK):
        cols = slice(c * DENSE_CHUNK, (c + 1) * DENSE_CHUNK)
        part = None
        for j in range(n_sub):
            w = jnp.concatenate([gate_piece(keys_per_sub * j + k, c) for k in range(keys_per_sub)], axis=0)
            d = jnp.dot(upT_ref[:, j * DENSE_SUB:(j + 1) * DENSE_SUB], w, preferred_element_type=F32)
            part = d if part is None else part + d
            act_piece(j, c)
        acc_ref[:, cols] += part


def _dense_kernel(h2_ref, down_ref, upT_ref, c2_ref, e2_ref, n1_ref, e1_ref, x1_ref, gate_ref, g_ref,
                  o_ref, acc_ref, act0_ref, act1_ref):
    f = pl.program_id(0)
    out_tile = jnp.clip(f - 1, 0, DENSE_TILES - 1) % DENSE_NE

    @pl.when(f == 0)
    def _():
        act1_ref[...] = jnp.zeros(act1_ref.shape, F32)

    @pl.when(out_tile == 0)
    def _():
        acc_ref[...] = jnp.zeros(acc_ref.shape, F32)

    stage = functools.partial(_dense_stage_body, f, h2_ref, down_ref, upT_ref, c2_ref, e2_ref, n1_ref, e1_ref,
                              acc_ref)

    @pl.when(f % 2 == 0)
    def _():
        stage(act0_ref, act1_ref)

    @pl.when(f % 2 == 1)
    def _():
        stage(act1_ref, act0_ref)

    @pl.when(jnp.logical_and(out_tile == DENSE_NE - 1, f >= 1))
    def _():
        x2 = x1_ref[...] + gate_ref[0] * acc_ref[...].T
        ms = jnp.mean(x2 * x2, axis=-1, keepdims=True)
        o_ref[...] = x2 * lax.rsqrt(ms + EPS) * g_ref[...]


def _dense(h2T, down, upT, c2, e2, n1, e1, x1, gate, final_g):
    tiles_per_batch = SEQ // DENSE_TT

    def tile(f, lag):
        return jnp.clip(f - lag, 0, DENSE_TILES - 1)

    plan = lambda f: (0, tile(f, 1) // DENSE_NE, 0, 0)
    out_tok = lambda f: (tile(f, 1) // DENSE_NE, 0)
    plan_blk = (PEER_HEADS, DENSE_TT // LANES, PEER_KEYS, LANES)
    rank_map = lambda f: (0, 0, tile(f, 1) // DENSE_NE)
    rank_blk = (PEER_HEADS, PEER_KEYS, DENSE_TT)
    return pl.pallas_call(
        _dense_kernel,
        grid=(DENSE_TILES + 1,),
        in_specs=[
            pl.BlockSpec((D_MODEL, DENSE_TT), lambda f: (0, tile(f, 0) // DENSE_NE)),
            pl.BlockSpec((DENSE_TE, D_MODEL), lambda f: (tile(f, 0) % DENSE_NE, 0)),
            pl.BlockSpec((D_MODEL, DENSE_TE), lambda f: (0, tile(f, 1) % DENSE_NE)),
            pl.BlockSpec(rank_blk, rank_map),
            pl.BlockSpec(rank_blk, rank_map),
            pl.BlockSpec(plan_blk, plan),
            pl.BlockSpec(plan_blk, plan),
            pl.BlockSpec((DENSE_TT, D_MODEL), out_tok),
            pl.BlockSpec((1, 1, D_MODEL), lambda f: (tile(f, 1) // DENSE_NE // tiles_per_batch, 0, 0)),
            pl.BlockSpec((1, D_MODEL), lambda f: (0, 0)),
        ],
        out_specs=pl.BlockSpec((DENSE_TT, D_MODEL), out_tok),
        out_shape=jax.ShapeDtypeStruct((TOKENS, D_MODEL), F32),
        scratch_shapes=[
            pltpu.VMEM((D_MODEL, DENSE_TT), F32),
            pltpu.VMEM((DENSE_TE, DENSE_TT), F32),
            pltpu.VMEM((DENSE_TE, DENSE_TT), F32),
        ],
        compiler_params=_cparams(("arbitrary",)),
        name="peer_dense",
    )(h2T, down, upT, c2, e2, n1, e1, x1, gate, final_g)


def _transpose_kernel(x_ref, o_ref):
    o_ref[...] = x_ref[...].T.astype(BF16)


def _transpose_table(t):
    rows, cols = t.shape
    tr = 512
    return pl.pallas_call(
        _transpose_kernel,
        grid=(rows // tr,),
        in_specs=[pl.BlockSpec((tr, cols), lambda i: (i, 0))],
        out_specs=pl.BlockSpec((cols, tr), lambda i: (0, i)),
        out_shape=jax.ShapeDtypeStruct((cols, rows), BF16),
        compiler_params=_cparams(("arbitrary",)),
        name="table_transpose",
    )(t)


def _position_dft_tables():
    radix = 64
    r = lax.broadcasted_iota(jnp.int32, (radix, SEQ), 0)
    k = lax.broadcasted_iota(jnp.int32, (radix, SEQ), 1)
    coarse = ((r * k) % radix).astype(F32) * (2.0 * math.pi / radix)
    fine = ((r * k) % SEQ).astype(F32) * (2.0 * math.pi / SEQ)
    scale = 1.0 / math.sqrt(SEQ)
    ca, sa = jnp.cos(coarse)[:, None, :] * scale, jnp.sin(coarse)[:, None, :] * scale
    cb, sb = jnp.cos(fine)[None, :, :], jnp.sin(fine)[None, :, :]
    cos_tab = (ca * cb - sa * sb).reshape(SEQ, SEQ).astype(BF16)
    sin_tab = (sa * cb + ca * sb).reshape(SEQ, SEQ).astype(BF16)
    return cos_tab, sin_tab


def _channel_dft_table():
    j = lax.broadcasted_iota(jnp.int32, (FNET_GROUP, FNET_GROUP), 0)
    k = lax.broadcasted_iota(jnp.int32, (FNET_GROUP, FNET_GROUP), 1)
    ang = ((j * k) % FNET_GROUP).astype(F32) * (2.0 * math.pi / FNET_GROUP)
    scale = 1.0 / math.sqrt(FNET_GROUP)
    return jnp.concatenate([jnp.cos(ang) * scale, jnp.sin(ang) * scale], axis=1).astype(BF16)


def _pad_lanes(v, width):
    return jnp.pad(v, ((0, 0), (0, width - v.shape[1])))


def kernel(x, c, w_ada, b_ada, norm_mix_g, w_in, conv_w, conv_b, a_log_fwd, a_log_bwd, dt_bias_fwd, dt_bias_bwd, d_skip, ssm_norm_g, w_out, norm_ffn_g, w_query, sub_keys, expert_down, expert_up, final_norm_g):
    assert w_ada.shape[0] == 1, "single-layer problem: the final RMSNorm is fused into the PEER kernel"
    xt = x.reshape(TOKENS, D_MODEL)
    c_pad = jnp.pad(c, ((0, SUBLANES - BATCH), (0, 0)))
    cs_tab, ss_tab = _position_dft_tables()
    csc_tab = _channel_dft_table()
    for layer in range(1):
        mod = _adaln(c_pad, w_ada[layer], b_ada[layer][None, :])[:BATCH]
        shift_m, scale_m, gate_m, shift_f, scale_f, gate_f = [
            m.reshape(BATCH, 1, D_MODEL) for m in jnp.split(mod, 6, axis=-1)]

        w = w_in[layer]
        wf = w[:, :D_FNET].astype(BF16)
        wz = w[:, D_FNET:D_FNET + D_SSM].astype(BF16)
        wx = w[:, D_FNET + D_SSM:D_FNET + D_SSM + D_CONV].astype(BF16)
        wdt = _pad_lanes(w[:, D_FNET + D_SSM + D_CONV:], DT_PAD).astype(BF16)
        uc, us, z, xbc, dt, dtT = _inproj(xt, norm_mix_g[layer][None, :], shift_m, scale_m,
                                          wf, wz, wx, wdt, wdt.T, csc_tab)
        y_fnet = _dft(cs_tab, ss_tab, uc, us)

        xconv = _conv(xbc.reshape(BATCH, SEQ, D_CONV), conv_w[layer], conv_b[layer][None, :])
        alog_row = _pad_lanes(jnp.concatenate([a_log_fwd[layer], a_log_bwd[layer]])[None, :], DT_PAD)
        bias_row = _pad_lanes(jnp.concatenate([dt_bias_fwd[layer], dt_bias_bwd[layer]])[None, :], DT_PAD)
        dskip_row = jnp.repeat(d_skip[layer], SSM_HEAD_DIM)[None, :]
        y_ssm = _ssd(xconv.reshape(TOKENS, D_CONV), dt, dtT, z, alog_row, bias_row, alog_row.T, bias_row.T,
                     dskip_row, ssm_norm_g[layer][None, :])

        wo = w_out[layer].astype(BF16)
        x1, h2, h2T = _outproj(xt, y_fnet, y_ssm, wo[:D_FNET], wo[D_FNET:], gate_m,
                          norm_ffn_g[layer][None, :], shift_f, scale_f)

        keys = sub_keys[layer].reshape(2 * PEER_HEADS, PEER_KEYS, PEER_HALF).astype(BF16)
        c2, e2, n1, e1 = _plan(h2, w_query[layer].astype(BF16), keys)
        xt = _dense(h2T, expert_down[layer].astype(BF16), _transpose_table(expert_up[layer]),
                    c2, e2, n1, e1, x1, gate_f, final_norm_g[None, :])
    return xt.reshape(BATCH, SEQ, D_MODEL)
```

```python
import functools
import math

import jax
import jax.numpy as jnp
from jax import lax
from jax.experimental import pallas as pl
from jax.experimental.pallas import tpu as pltpu

F32 = jnp.float32
BF16 = jnp.bfloat16
HIGHEST = lax.Precision.HIGHEST

D_MODEL = 1024
BATCH = 4
SEQ = 4096
TOKENS = BATCH * SEQ
D_MIX = 2 * D_MODEL
D_FNET = D_MIX // 4
N_FNET_GROUPS = 4
FNET_GROUP = D_FNET // N_FNET_GROUPS
D_SSM = D_MIX - D_FNET
SSM_HEAD_DIM = 64
N_SSM_HEADS = D_SSM // SSM_HEAD_DIM
N_BC_GROUPS = 4
HEADS_PER_GROUP = N_SSM_HEADS // N_BC_GROUPS
D_STATE = 128
CONV_WIDTH = 5
SSD_CHUNK = 128
N_CHUNKS = SEQ // SSD_CHUNK
SSD_STEP_CHUNKS = 2
D_BC = N_BC_GROUPS * D_STATE
D_CONV = D_SSM + 2 * D_BC
GROUP_WIDTH = D_SSM // N_BC_GROUPS
PEER_HEADS = 8
PEER_KEYS = 128
PEER_EXPERTS = PEER_KEYS * PEER_KEYS
PEER_HALF = 128
PEER_TOPK = 16
EPS = 1e-6

LANES = 128
SUBLANES = 8
BF16_ROWS = 16
DT_PAD = LANES
VMEM_LIMIT = 56 * 1024 * 1024

ADA_TN = 512
PROJ_TM = 512
DFT_TM = 512
CONV_TC = 256
CONV_TR = 512
CONV_HALO = SUBLANES
OUT_TM = 512
PLAN_TT = 256
DENSE_TT = 512
DENSE_TE = 2048
DENSE_SUB = 256
DENSE_CHUNK = 256


def _cparams(sem):
    return pltpu.CompilerParams(dimension_semantics=sem, vmem_limit_bytes=VMEM_LIMIT)


def _sigmoid(x):
    return 1.0 / (1.0 + jnp.exp(-x))


def _bf16_terms(x):
    terms = []
    rest = x
    for _ in range(3):
        term = rest.astype(BF16)
        terms.append(term)
        rest = rest - term.astype(F32)
    return terms


def _softplus(x):
    return jnp.maximum(x, 0.0) + jnp.log(1.0 + jnp.exp(-jnp.abs(x)))


def _adaln_kernel(c_ref, w_ref, b_ref, o_ref):
    c = c_ref[...]
    ca = c * _sigmoid(c)
    o_ref[...] = jnp.dot(ca, w_ref[...], preferred_element_type=F32, precision=HIGHEST) + b_ref[...]


def _adaln(c_pad, w_ada, b_ada):
    n = w_ada.shape[1]
    return pl.pallas_call(
        _adaln_kernel,
        grid=(n // ADA_TN,),
        in_specs=[
            pl.BlockSpec((SUBLANES, D_MODEL), lambda j: (0, 0)),
            pl.BlockSpec((D_MODEL, ADA_TN), lambda j: (0, j)),
            pl.BlockSpec((1, ADA_TN), lambda j: (0, j)),
        ],
        out_specs=pl.BlockSpec((SUBLANES, ADA_TN), lambda j: (0, j)),
        out_shape=jax.ShapeDtypeStruct((SUBLANES, n), F32),
        compiler_params=_cparams(("arbitrary",)),
        name="adaln",
    )(c_pad, w_ada, b_ada)


def _modulated_norm(x, gain, shift, scale):
    ms = jnp.mean(x * x, axis=-1, keepdims=True)
    xn = x * lax.rsqrt(ms + EPS) * gain
    return xn * (1.0 + scale) + shift


def _inproj_kernel(x_ref, g_ref, shift_ref, scale_ref, wf_ref, wz_ref, wx_ref, wdt_ref, wdtT_ref, csc_ref,
                   uc_ref, us_ref, z_ref, xbc_ref, dt_ref, dtT_ref):
    h = _modulated_norm(x_ref[...], g_ref[...], shift_ref[0], scale_ref[0])
    hb = h.astype(BF16)
    fb = jnp.dot(hb, wf_ref[...], preferred_element_type=F32).astype(BF16)
    for g in range(N_FNET_GROUPS):
        sl = slice(g * FNET_GROUP, (g + 1) * FNET_GROUP)
        u = jnp.dot(fb[:, sl], csc_ref[...], preferred_element_type=F32)
        uc_ref[:, sl] = u[:, :FNET_GROUP].astype(BF16)
        us_ref[:, sl] = u[:, FNET_GROUP:].astype(BF16)
    z_ref[...] = jnp.dot(hb, wz_ref[...], preferred_element_type=F32).astype(BF16)
    xbc_ref[...] = jnp.dot(hb, wx_ref[...], preferred_element_type=F32).astype(BF16)
    dt_ref[...] = jnp.dot(hb, wdt_ref[...], preferred_element_type=F32)
    dtT_ref[...] = lax.dot_general(wdtT_ref[...], hb, (((1,), (1,)), ((), ())), preferred_element_type=F32)


def _inproj(x2d, gain, shift, scale, wf, wz, wx, wdt, wdtT, csc):
    tiles_per_batch = SEQ // PROJ_TM
    row = lambda i: (i, 0)
    const = lambda i: (0, 0)
    per_batch = lambda i: (i // tiles_per_batch, 0, 0)
    return pl.pallas_call(
        _inproj_kernel,
        grid=(TOKENS // PROJ_TM,),
        in_specs=[
            pl.BlockSpec((PROJ_TM, D_MODEL), row),
            pl.BlockSpec((1, D_MODEL), const),
            pl.BlockSpec((1, 1, D_MODEL), per_batch),
            pl.BlockSpec((1, 1, D_MODEL), per_batch),
            pl.BlockSpec(wf.shape, const),
            pl.BlockSpec(wz.shape, const),
            pl.BlockSpec(wx.shape, const),
            pl.BlockSpec(wdt.shape, const),
            pl.BlockSpec(wdtT.shape, const),
            pl.BlockSpec(csc.shape, const),
        ],
        out_specs=[
            pl.BlockSpec((PROJ_TM, D_FNET), row),
            pl.BlockSpec((PROJ_TM, D_FNET), row),
            pl.BlockSpec((PROJ_TM, D_SSM), row),
            pl.BlockSpec((PROJ_TM, D_CONV), row),
            pl.BlockSpec((PROJ_TM, DT_PAD), row),
            pl.BlockSpec((DT_PAD, PROJ_TM), lambda i: (0, i)),
        ],
        out_shape=[
            jax.ShapeDtypeStruct((TOKENS, D_FNET), BF16),
            jax.ShapeDtypeStruct((TOKENS, D_FNET), BF16),
            jax.ShapeDtypeStruct((TOKENS, D_SSM), BF16),
            jax.ShapeDtypeStruct((TOKENS, D_CONV), BF16),
            jax.ShapeDtypeStruct((TOKENS, DT_PAD), F32),
            jax.ShapeDtypeStruct((DT_PAD, TOKENS), F32),
        ],
        compiler_params=_cparams(("arbitrary",)),
        name="inproj",
    )(x2d, gain, shift, scale, wf, wz, wx, wdt, wdtT, csc)


def _dft_kernel(cs_ref, ss_ref, uc_ref, us_ref, o_ref):
    y = jnp.dot(cs_ref[...], uc_ref[...], preferred_element_type=F32)
    y = y - jnp.dot(ss_ref[...], us_ref[...], preferred_element_type=F32)
    o_ref[...] = y.astype(BF16)


def _dft(cs, ss, uc, us):
    mt = SEQ // DFT_TM
    return pl.pallas_call(
        _dft_kernel,
        grid=(BATCH, mt),
        in_specs=[
            pl.BlockSpec((DFT_TM, SEQ), lambda b, m: (m, 0)),
            pl.BlockSpec((DFT_TM, SEQ), lambda b, m: (m, 0)),
            pl.BlockSpec((SEQ, D_FNET), lambda b, m: (b, 0)),
            pl.BlockSpec((SEQ, D_FNET), lambda b, m: (b, 0)),
        ],
        out_specs=pl.BlockSpec((DFT_TM, D_FNET), lambda b, m: (b * mt + m, 0)),
        out_shape=jax.ShapeDtypeStruct((TOKENS, D_FNET), BF16),
        compiler_params=_cparams(("arbitrary", "arbitrary")),
        name="fnet_dft",
    )(cs, ss, uc, us)


def _conv_kernel(x_ref, w_ref, b_ref, o_ref, pad_ref):
    zeros = jnp.zeros((CONV_HALO, CONV_TC), F32)
    pad_ref[0:CONV_HALO, :] = zeros
    pad_ref[SEQ + CONV_HALO:SEQ + 2 * CONV_HALO, :] = zeros
    pad_ref[CONV_HALO:SEQ + CONV_HALO, :] = x_ref[0].astype(F32)
    w = w_ref[...]
    bias = b_ref[...]
    first = CONV_HALO - CONV_WIDTH // 2
    for i in range(SEQ // CONV_TR):
        acc = jnp.broadcast_to(bias, (CONV_TR, CONV_TC))
        for k in range(CONV_WIDTH):
            lo = first + k + i * CONV_TR
            acc = acc + pad_ref[lo:lo + CONV_TR, :] * w[k:k + 1, :]
        o_ref[0, i * CONV_TR:(i + 1) * CONV_TR, :] = (acc * _sigmoid(acc)).astype(BF16)


def _conv(xbc3, conv_w, conv_b):
    return pl.pallas_call(
        _conv_kernel,
        grid=(BATCH, D_CONV // CONV_TC),
        in_specs=[
            pl.BlockSpec((1, SEQ, CONV_TC), lambda b, j: (b, 0, j)),
            pl.BlockSpec((CONV_WIDTH, CONV_TC), lambda b, j: (0, j)),
            pl.BlockSpec((1, CONV_TC), lambda b, j: (0, j)),
        ],
        out_specs=pl.BlockSpec((1, SEQ, CONV_TC), lambda b, j: (b, 0, j)),
        out_shape=jax.ShapeDtypeStruct((BATCH, SEQ, D_CONV), BF16),
        scratch_shapes=[pltpu.VMEM((SEQ + 2 * CONV_HALO, CONV_TC), F32)],
        compiler_params=_cparams(("arbitrary", "arbitrary")),
        name="conv_silu",
    )(xbc3, conv_w, conv_b)


def _ssd_direction(xc, dtvT, cs, csT, a, aT, decay_row, state_ref, yacc_ref, head_base, forward):
    li = lax.broadcasted_iota(jnp.int32, (SSD_CHUNK, SSD_CHUNK), 0)
    si = lax.broadcasted_iota(jnp.int32, (SSD_CHUNK, SSD_CHUNK), 1)
    low_half = si < SSM_HEAD_DIM
    if forward:
        u, uT = cs, csT
        mask = li >= si
    else:
        u, uT = cs - a, csT - aT
        mask = si >= li
    pair = 2 * SSM_HEAD_DIM
    for g in range(N_BC_GROUPS):
        bc = xc[:, D_SSM + g * D_STATE:D_SSM + (g + 1) * D_STATE]
        cc = xc[:, D_SSM + D_BC + g * D_STATE:D_SSM + D_BC + (g + 1) * D_STATE]
        gmat = lax.dot_general(cc, bc, (((1,), (1,)), ((), ())), preferred_element_type=F32)
        bcT = bc.astype(F32).T
        ccf = cc.astype(F32)
        for pr in range(HEADS_PER_GROUP // 2):
            h0 = g * HEADS_PER_GROUP + 2 * pr
            xcols = slice(h0 * SSM_HEAD_DIM, h0 * SSM_HEAD_DIM + pair)
            scols = slice(pr * pair, (pr + 1) * pair)
            xpair = xc[:, xcols]
            spair = state_ref[g, :, scols]
            rhs = jnp.concatenate([xpair, spair.astype(BF16)], axis=0)
            ys, upds = [], []
            for j in range(2):
                hh = head_base + h0 + j
                ub = jnp.broadcast_to(u[:, hh:hh + 1], (SSD_CHUNK, SSD_CHUNK))
                urow = uT[hh:hh + 1, :]
                dtrow = dtvT[hh:hh + 1, :]
                tot = cs[SSD_CHUNK - 1:SSD_CHUNK, hh:hh + 1]
                if forward:
                    diff = ub - urow
                    cscale = jnp.exp(ub)
                    wrow = jnp.exp(tot - urow) * dtrow
                else:
                    diff = urow - ub
                    cscale = jnp.exp(tot - ub)
                    wrow = jnp.exp(urow) * dtrow
                lmat = jnp.where(mask, jnp.exp(diff), 0.0)
                lhs = jnp.concatenate([(gmat * lmat * dtrow).astype(BF16), (ccf * cscale).astype(BF16)], axis=1)
                ys.append(jnp.dot(lhs, rhs, preferred_element_type=F32))
                upds.append(jnp.dot((bcT * wrow).astype(BF16), xpair, preferred_element_type=F32))
            yacc_ref[:, xcols] = jnp.where(low_half, ys[0], ys[1])
            state_ref[g, :, scols] = spair * decay_row[:, xcols] + jnp.where(low_half, upds[0], upds[1])


def _ssd_kernel(xc_ref, dt_ref, dtT_ref, z_ref, alog_row_ref, bias_row_ref, alog_col_ref, bias_col_ref,
                dskip_ref, ng_ref, expand_ref, y_ref, state_ref, yacc_ref, yb_ref):
    phase = pl.program_id(1)
    step = pl.program_id(2)

    @pl.when(step == 0)
    def _():
        state_ref[...] = jnp.zeros(state_ref.shape, F32)

    li = lax.broadcasted_iota(jnp.int32, (SSD_CHUNK, SSD_CHUNK), 0)
    si = lax.broadcasted_iota(jnp.int32, (SSD_CHUNK, SSD_CHUNK), 1)
    lower = (li >= si).astype(BF16)
    upper = (li <= si).astype(BF16)
    used = 2 * N_SSM_HEADS

    def scan_chunk(sub, forward):
        rows = slice(sub * SSD_CHUNK, (sub + 1) * SSD_CHUNK)
        xc = xc_ref[rows, :]
        dtv = _softplus(dt_ref[rows, :] + bias_row_ref[...])
        a = dtv * (-jnp.exp(alog_row_ref[...]))
        dtvT = _softplus(dtT_ref[0:used, rows] + bias_col_ref[0:used, :])
        aT = dtvT * (-jnp.exp(alog_col_ref[0:used, :]))
        cs = sum(jnp.dot(lower, t, preferred_element_type=F32) for t in _bf16_terms(a))
        csT = sum(jnp.dot(t, upper, preferred_element_type=F32) for t in _bf16_terms(aT))
        chunk_decay = jnp.exp(cs[SSD_CHUNK - 1:SSD_CHUNK, :])
        expand = expand_ref[0 if forward else 1]
        decay_row = sum(jnp.dot(t, expand, preferred_element_type=F32) for t in _bf16_terms(chunk_decay))
        _ssd_direction(xc, dtvT, cs, csT, a, aT, decay_row, state_ref, yacc_ref,
                       0 if forward else N_SSM_HEADS, forward)
        return rows, xc

    @pl.when(phase == 0)
    def _():
        for sub in reversed(range(SSD_STEP_CHUNKS)):
            scan_chunk(sub, False)
            chunk = (N_CHUNKS // SSD_STEP_CHUNKS - 1 - step) * SSD_STEP_CHUNKS + sub
            yb_ref[chunk] = yacc_ref[...].astype(BF16)

    @pl.when(phase == 1)
    def _():
        for sub in range(SSD_STEP_CHUNKS):
            rows, xc = scan_chunk(sub, True)
            xs = xc[:, :D_SSM].astype(F32)
            y = yacc_ref[...] + yb_ref[step * SSD_STEP_CHUNKS + sub].astype(F32) + xs * dskip_ref[...]
            z = z_ref[rows, :].astype(F32)
            y = y * (z * _sigmoid(z))
            for g in range(N_BC_GROUPS):
                gs = slice(g * GROUP_WIDTH, (g + 1) * GROUP_WIDTH)
                yg = y[:, gs]
                ms = jnp.mean(yg * yg, axis=-1, keepdims=True)
                y_ref[rows, gs] = (yg * lax.rsqrt(ms + EPS) * ng_ref[:, gs]).astype(BF16)


def _head_expansion():
    col = lax.broadcasted_iota(jnp.int32, (2, DT_PAD, D_SSM), 1)
    head = lax.broadcasted_iota(jnp.int32, (2, DT_PAD, D_SSM), 2) // SSM_HEAD_DIM
    direction = lax.broadcasted_iota(jnp.int32, (2, DT_PAD, D_SSM), 0)
    return (col == direction * N_SSM_HEADS + head).astype(BF16)


def _ssd(xconv, dt, dtT, z, alog_row, bias_row, alog_col, bias_col, dskip_row, norm_g):
    steps = N_CHUNKS // SSD_STEP_CHUNKS
    step_rows = SSD_STEP_CHUNKS * SSD_CHUNK

    def block_of(p, s):
        return p * s + (1 - p) * (steps - 1 - s)

    rows = lambda b, p, s: (b * steps + block_of(p, s), 0)
    cols = lambda b, p, s: (0, b * steps + block_of(p, s))
    fwd_only = lambda b, p, s: (b * steps + p * s, 0)
    const = lambda b, p, s: (0, 0)
    return pl.pallas_call(
        _ssd_kernel,
        grid=(BATCH, 2, steps),
        in_specs=[
            pl.BlockSpec((step_rows, D_CONV), rows),
            pl.BlockSpec((step_rows, DT_PAD), rows),
            pl.BlockSpec((DT_PAD, step_rows), cols),
            pl.BlockSpec((step_rows, D_SSM), fwd_only),
            pl.BlockSpec((1, DT_PAD), const),
            pl.BlockSpec((1, DT_PAD), const),
            pl.BlockSpec((DT_PAD, 1), const),
            pl.BlockSpec((DT_PAD, 1), const),
            pl.BlockSpec((1, D_SSM), const),
            pl.BlockSpec((1, D_SSM), const),
            pl.BlockSpec((2, DT_PAD, D_SSM), lambda b, p, s: (0, 0, 0)),
        ],
        out_specs=pl.BlockSpec((step_rows, D_SSM), fwd_only),
        out_shape=jax.ShapeDtypeStruct((TOKENS, D_SSM), BF16),
        scratch_shapes=[
            pltpu.VMEM((N_BC_GROUPS, D_STATE, GROUP_WIDTH), F32),
            pltpu.VMEM((SSD_CHUNK, D_SSM), F32),
            pltpu.VMEM((N_CHUNKS, SSD_CHUNK, D_SSM), BF16),
        ],
        compiler_params=_cparams(("arbitrary", "arbitrary", "arbitrary")),
        name="ssd_scan",
    )(xconv, dt, dtT, z, alog_row, bias_row, alog_col, bias_col, dskip_row, norm_g, _head_expansion())


def _outproj_kernel(x_ref, yf_ref, ys_ref, wf_ref, ws_ref, gate_ref, g_ref, shift_ref, scale_ref,
                    x1_ref, h2_ref, h2T_ref):
    mix = jnp.dot(yf_ref[...], wf_ref[...], preferred_element_type=F32)
    mix = mix + jnp.dot(ys_ref[...], ws_ref[...], preferred_element_type=F32)
    x1 = x_ref[...] + gate_ref[0] * mix
    x1_ref[...] = x1
    h2 = _modulated_norm(x1, g_ref[...], shift_ref[0], scale_ref[0])
    h2_ref[...] = h2.astype(BF16)
    h2T_ref[...] = h2.T.astype(BF16)


def _outproj(x2d, yf, ys, wf, ws, gate, gain, shift, scale):
    tiles_per_batch = SEQ // OUT_TM
    row = lambda i: (i, 0)
    const = lambda i: (0, 0)
    per_batch = lambda i: (i // tiles_per_batch, 0, 0)
    return pl.pallas_call(
        _outproj_kernel,
        grid=(TOKENS // OUT_TM,),
        in_specs=[
            pl.BlockSpec((OUT_TM, D_MODEL), row),
            pl.BlockSpec((OUT_TM, D_FNET), row),
            pl.BlockSpec((OUT_TM, D_SSM), row),
            pl.BlockSpec(wf.shape, const),
            pl.BlockSpec(ws.shape, const),
            pl.BlockSpec((1, 1, D_MODEL), per_batch),
            pl.BlockSpec((1, D_MODEL), const),
            pl.BlockSpec((1, 1, D_MODEL), per_batch),
            pl.BlockSpec((1, 1, D_MODEL), per_batch),
        ],
        out_specs=[pl.BlockSpec((OUT_TM, D_MODEL), row), pl.BlockSpec((OUT_TM, D_MODEL), row),
                   pl.BlockSpec((D_MODEL, OUT_TM), lambda i: (0, i))],
        out_shape=[jax.ShapeDtypeStruct((TOKENS, D_MODEL), F32), jax.ShapeDtypeStruct((TOKENS, D_MODEL), BF16),
                   jax.ShapeDtypeStruct((D_MODEL, TOKENS), BF16)],
        compiler_params=_cparams(("arbitrary",)),
        name="outproj",
    )(x2d, yf, ys, wf, ws, gate, gain, shift, scale)


MARKER_UNIT = 2.0 ** 100


def _extract_topk_marked(x, k):
    work = x
    vals = []
    for r in range(k):
        m = jnp.max(work, axis=0, keepdims=True)
        work = jnp.where(work == m, -(k + r) * MARKER_UNIT, work)
        vals.append(m)
    marked = work <= -k * MARKER_UNIT
    rank = jnp.where(marked, work * (-1.0 / MARKER_UNIT) - float(k), float(k))
    in_range = jnp.min(x, axis=0, keepdims=True) > -MARKER_UNIT
    return jnp.concatenate(vals, axis=0), rank, in_range


def _extract_topk(x, k, break_ties):
    rows = x.shape[0]
    iota = lax.broadcasted_iota(jnp.int32, x.shape, 0).astype(F32)
    rank = jnp.full(x.shape, float(k), F32)
    work = x
    vals = []
    for r in range(k):
        m = jnp.max(work, axis=0, keepdims=True)
        sel = work == m
        if break_ties:
            first = jnp.min(jnp.where(sel, iota, float(rows)), axis=0, keepdims=True)
            sel = iota == first
        rank = jnp.where(sel, float(r), rank)
        work = jnp.where(sel, -jnp.inf, work)
        vals.append(m)
    return jnp.concatenate(vals, axis=0), rank


def _candidate_width(r):
    return PEER_TOPK // (r + 1)


def _candidates(v1, v2):
    sub = lax.broadcasted_iota(jnp.int32, (SUBLANES, v1.shape[1]), 0)
    blocks = [v1[0:1] + v2, v1[1:2] + v2[0:SUBLANES]]
    for r in range(2, SUBLANES):
        blocks.append(jnp.where(sub < _candidate_width(r), v1[r:r + 1] + v2[0:SUBLANES], -jnp.inf))
    blocks.append(v1[SUBLANES:] + v2[0:1])
    return jnp.concatenate(blocks, axis=0)


def _row_counts(chosen):
    counts = [jnp.sum(chosen[0:PEER_TOPK], axis=0, keepdims=True)]
    for r in range(1, SUBLANES):
        lo = PEER_TOPK + (r - 1) * SUBLANES
        counts.append(jnp.sum(chosen[lo:lo + SUBLANES], axis=0, keepdims=True))
    tail = chosen[PEER_TOPK + (SUBLANES - 1) * SUBLANES:]
    counts.extend(tail[j:j + 1] for j in range(SUBLANES))
    return counts


def _plan_head(s1, s2, break_ties):
    if break_ties:
        v1, r1 = _extract_topk(s1, PEER_TOPK, True)
        v2, r2 = _extract_topk(s2, PEER_TOPK, True)
        in_range = None
    else:
        v1, r1, ok1 = _extract_topk_marked(s1, PEER_TOPK)
        v2, r2, ok2 = _extract_topk_marked(s2, PEER_TOPK)
        in_range = jnp.logical_and(ok1, ok2)
    cand = _candidates(v1, v2)
    _, crank = _extract_topk(cand, PEER_TOPK, break_ties)
    chosen = (crank < float(PEER_TOPK)).astype(F32)
    counts = _row_counts(chosen)
    top = v1[0:1] + v2[0:1]
    zsum = jnp.sum(chosen * jnp.exp(cand - top), axis=0, keepdims=True)
    r1b = r1.astype(BF16)
    n1 = jnp.zeros(s1.shape, BF16)
    for r in range(PEER_TOPK):
        n1 = jnp.where(r1b == float(r), counts[r].astype(BF16), n1)
    e1 = jnp.exp(s1 - v1[0:1]) / zsum
    e2 = jnp.exp(s2 - v2[0:1])
    ranked = (r1 < float(PEER_TOPK)).astype(F32) + (r2 < float(PEER_TOPK)).astype(F32)
    total = jnp.sum(ranked, axis=0, keepdims=True) + jnp.sum(chosen, axis=0, keepdims=True)
    good = total == 3.0 * PEER_TOPK
    if in_range is not None:
        good = jnp.logical_and(good, in_range)
    clean = jnp.min(jnp.where(good, 1.0, 0.0)) > 0.5
    return r2, e2, n1.astype(F32), e1, clean


def _twin_bf16_words(x):
    hi = pltpu.bitcast(x.astype(BF16).astype(F32), jnp.uint32)
    return hi | (hi >> 16)


def _plan_kernel(h2_ref, wq_ref, keys_ref, c2_ref, e2_ref, n1_ref, e1_ref, q_ref):
    q_ref[...] = jnp.dot(h2_ref[...], wq_ref[...], preferred_element_type=F32).astype(BF16)

    def per_head(h, carry):
        def scores(side):
            col = pl.multiple_of((2 * h + side) * PEER_HALF, PEER_HALF)
            qh = q_ref[:, pl.ds(col, PEER_HALF)]
            return lax.dot_general(keys_ref[2 * h + side], qh, (((1,), (1,)), ((), ())), preferred_element_type=F32)

        s1 = scores(0)
        s2 = scores(1)

        def store(r2, e2, n1, e1):
            c2_ref[h] = r2.astype(BF16)
            e2_ref[h] = e2.astype(BF16)
            for c in range(PLAN_TT // LANES):
                cols = slice(c * LANES, (c + 1) * LANES)
                n1_ref[h, c] = _twin_bf16_words(n1[:, cols])
                e1_ref[h, c] = _twin_bf16_words(e1[:, cols])

        r2, e2, n1, e1, clean = _plan_head(s1, s2, break_ties=False)
        store(r2, e2, n1, e1)

        @pl.when(jnp.logical_not(clean))
        def _():
            store(*_plan_head(s1, s2, break_ties=True)[:4])

        return carry

    lax.fori_loop(0, PEER_HEADS, per_head, 0)


def _plan(h2, wq, keys):
    blk = lambda i: (0, 0, i)
    shp = (PEER_HEADS, PEER_KEYS, TOKENS)
    lane_blk = lambda i: (0, i, 0, 0)
    lane_shp = (PEER_HEADS, TOKENS // LANES, PEER_KEYS, LANES)
    lane_spec = pl.BlockSpec((PEER_HEADS, PLAN_TT // LANES, PEER_KEYS, LANES), lane_blk)
    return pl.pallas_call(
        _plan_kernel,
        grid=(TOKENS // PLAN_TT,),
        in_specs=[
            pl.BlockSpec((PLAN_TT, D_MODEL), lambda i: (i, 0)),
            pl.BlockSpec(wq.shape, lambda i: (0, 0)),
            pl.BlockSpec(keys.shape, lambda i: (0, 0, 0)),
        ],
        out_specs=[pl.BlockSpec((PEER_HEADS, PEER_KEYS, PLAN_TT), blk)] * 2 + [lane_spec] * 2,
        out_shape=[
            jax.ShapeDtypeStruct(shp, BF16),
            jax.ShapeDtypeStruct(shp, BF16),
            jax.ShapeDtypeStruct(lane_shp, jnp.uint32),
            jax.ShapeDtypeStruct(lane_shp, jnp.uint32),
        ],
        scratch_shapes=[pltpu.VMEM((PLAN_TT, 2 * PEER_HEADS * PEER_HALF), BF16)],
        compiler_params=_cparams(("arbitrary",)),
        name="peer_plan",
    )(h2, wq, keys)


def _gelu_bf16(x):
    half_x = x.astype(BF16) * 0.5
    return half_x + half_x * lax.erf(x * (1.0 / math.sqrt(2.0))).astype(BF16)


DENSE_NE = PEER_EXPERTS // DENSE_TE
DENSE_NT = TOKENS // DENSE_TT
DENSE_TILES = DENSE_NT * DENSE_NE
DENSE_KEYS_PER_TILE = DENSE_TE // PEER_KEYS


def _dense_stage_body(f, h2_ref, down_ref, upT_ref, c2_ref, e2_ref, n1_ref, e1_ref, acc_ref, act_w, act_r):
    gate_tile = jnp.clip(f - 1, 0, DENSE_TILES - 1) % DENSE_NE

    def act_piece(q, c):
        rows = slice(q * DENSE_SUB, (q + 1) * DENSE_SUB)
        cols = slice(c * DENSE_CHUNK, (c + 1) * DENSE_CHUNK)
        act_w[rows, cols] = jnp.dot(down_ref[rows, :], h2_ref[:, cols], preferred_element_type=F32)

    def gate_piece(k, c):
        i1 = gate_tile * DENSE_KEYS_PER_TILE + k
        cols = slice(c * DENSE_CHUNK, (c + 1) * DENSE_CHUNK)
        rows = slice(k * PEER_KEYS, (k + 1) * PEER_KEYS)

        def row_tile(ref, h):
            tiles = []
            for t in range(DENSE_CHUNK // LANES):
                lane_tile = c * (DENSE_CHUNK // LANES) + t
                words = jnp.broadcast_to(ref[h, lane_tile, pl.ds(i1, 1), :], (SUBLANES, LANES))
                tiles.append(jnp.concatenate([pltpu.bitcast(words, BF16)] * (PEER_KEYS // BF16_ROWS), axis=0))
            return jnp.concatenate(tiles, axis=1)

        gates = jnp.zeros((PEER_KEYS, DENSE_CHUNK), BF16)
        for h in range(PEER_HEADS):
            hit = c2_ref[h, :, cols] < row_tile(n1_ref, h)
            gates = gates + jnp.where(hit, e2_ref[h, :, cols], jnp.zeros((), BF16)) * row_tile(e1_ref, h)
        return gates * _gelu_bf16(act_r[rows, cols])

    keys_per_sub = DENSE_SUB // PEER_KEYS
    n_sub = DENSE_TE // DENSE_SUB
    for c in range(DENSE_TT // DENSE_CHUNK):
        cols = slice(c * DENSE_CHUNK, (c + 1) * DENSE_CHUNK)
        part = None
        for j in range(n_sub):
            w = jnp.concatenate([gate_piece(keys_per_sub * j + k, c) for k in range(keys_per_sub)], axis=0)
            d = jnp.dot(upT_ref[:, j * DENSE_SUB:(j + 1) * DENSE_SUB], w, preferred_element_type=F32)
            part = d if part is None else part + d
            act_piece(j, c)
        acc_ref[:, cols] += part


def _dense_kernel(h2_ref, down_ref, upT_ref, c2_ref, e2_ref, n1_ref, e1_ref, x1_ref, gate_ref, g_ref,
                  o_ref, acc_ref, act0_ref, act1_ref):
    f = pl.program_id(0)
    out_tile = jnp.clip(f - 1, 0, DENSE_TILES - 1) % DENSE_NE

    @pl.when(f == 0)
    def _():
        act1_ref[...] = jnp.zeros(act1_ref.shape, F32)

    @pl.when(out_tile == 0)
    def _():
        acc_ref[...] = jnp.zeros(acc_ref.shape, F32)

    stage = functools.partial(_dense_stage_body, f, h2_ref, down_ref, upT_ref, c2_ref, e2_ref, n1_ref, e1_ref,
                              acc_ref)

    @pl.when(f % 2 == 0)
    def _():
        stage(act0_ref, act1_ref)

    @pl.when(f % 2 == 1)
    def _():
        stage(act1_ref, act0_ref)

    @pl.when(jnp.logical_and(out_tile == DENSE_NE - 1, f >= 1))
    def _():
        x2 = x1_ref[...] + gate_ref[0] * acc_ref[...].T
        ms = jnp.mean(x2 * x2, axis=-1, keepdims=True)
        o_ref[...] = x2 * lax.rsqrt(ms + EPS) * g_ref[...]


def _dense(h2T, down, upT, c2, e2, n1, e1, x1, gate, final_g):
    tiles_per_batch = SEQ // DENSE_TT

    def tile(f, lag):
        return jnp.clip(f - lag, 0, DENSE_TILES - 1)

    plan = lambda f: (0, tile(f, 1) // DENSE_NE, 0, 0)
    out_tok = lambda f: (tile(f, 1) // DENSE_NE, 0)
    plan_blk = (PEER_HEADS, DENSE_TT // LANES, PEER_KEYS, LANES)
    rank_map = lambda f: (0, 0, tile(f, 1) // DENSE_NE)
    rank_blk = (PEER_HEADS, PEER_KEYS, DENSE_TT)
    return pl.pallas_call(
        _dense_kernel,
        grid=(DENSE_TILES + 1,),
        in_specs=[
            pl.BlockSpec((D_MODEL, DENSE_TT), lambda f: (0, tile(f, 0) // DENSE_NE)),
            pl.BlockSpec((DENSE_TE, D_MODEL), lambda f: (tile(f, 0) % DENSE_NE, 0)),
            pl.BlockSpec((D_MODEL, DENSE_TE), lambda f: (0, tile(f, 1) % DENSE_NE)),
            pl.BlockSpec(rank_blk, rank_map),
            pl.BlockSpec(rank_blk, rank_map),
            pl.BlockSpec(plan_blk, plan),
            pl.BlockSpec(plan_blk, plan),
            pl.BlockSpec((DENSE_TT, D_MODEL), out_tok),
            pl.BlockSpec((1, 1, D_MODEL), lambda f: (tile(f, 1) // DENSE_NE // tiles_per_batch, 0, 0)),
            pl.BlockSpec((1, D_MODEL), lambda f: (0, 0)),
        ],
        out_specs=pl.BlockSpec((DENSE_TT, D_MODEL), out_tok),
        out_shape=jax.ShapeDtypeStruct((TOKENS, D_MODEL), F32),
        scratch_shapes=[
            pltpu.VMEM((D_MODEL, DENSE_TT), F32),
            pltpu.VMEM((DENSE_TE, DENSE_TT), F32),
            pltpu.VMEM((DENSE_TE, DENSE_TT), F32),
        ],
        compiler_params=_cparams(("arbitrary",)),
        name="peer_dense",
    )(h2T, down, upT, c2, e2, n1, e1, x1, gate, final_g)


def _transpose_kernel(x_ref, o_ref):
    o_ref[...] = x_ref[...].T.astype(BF16)


def _transpose_table(t):
    rows, cols = t.shape
    tr = 512
    return pl.pallas_call(
        _transpose_kernel,
        grid=(rows // tr,),
        in_specs=[pl.BlockSpec((tr, cols), lambda i: (i, 0))],
        out_specs=pl.BlockSpec((cols, tr), lambda i: (0, i)),
        out_shape=jax.ShapeDtypeStruct((cols, rows), BF16),
        compiler_params=_cparams(("arbitrary",)),
        name="table_transpose",
    )(t)


def _position_dft_tables():
    radix = 64
    r = lax.broadcasted_iota(jnp.int32, (radix, SEQ), 0)
    k = lax.broadcasted_iota(jnp.int32, (radix, SEQ), 1)
    coarse = ((r * k) % radix).astype(F32) * (2.0 * math.pi / radix)
    fine = ((r * k) % SEQ).astype(F32) * (2.0 * math.pi / SEQ)
    scale = 1.0 / math.sqrt(SEQ)
    ca, sa = jnp.cos(coarse)[:, None, :] * scale, jnp.sin(coarse)[:, None, :] * scale
    cb, sb = jnp.cos(fine)[None, :, :], jnp.sin(fine)[None, :, :]
    cos_tab = (ca * cb - sa * sb).reshape(SEQ, SEQ).astype(BF16)
    sin_tab = (sa * cb + ca * sb).reshape(SEQ, SEQ).astype(BF16)
    return cos_tab, sin_tab


def _channel_dft_table():
    j = lax.broadcasted_iota(jnp.int32, (FNET_GROUP, FNET_GROUP), 0)
    k = lax.broadcasted_iota(jnp.int32, (FNET_GROUP, FNET_GROUP), 1)
    ang = ((j * k) % FNET_GROUP).astype(F32) * (2.0 * math.pi / FNET_GROUP)
    scale = 1.0 / math.sqrt(FNET_GROUP)
    return jnp.concatenate([jnp.cos(ang) * scale, jnp.sin(ang) * scale], axis=1).astype(BF16)


def _pad_lanes(v, width):
    return jnp.pad(v, ((0, 0), (0, width - v.shape[1])))


def kernel(x, c, w_ada, b_ada, norm_mix_g, w_in, conv_w, conv_b, a_log_fwd, a_log_bwd, dt_bias_fwd, dt_bias_bwd, d_skip, ssm_norm_g, w_out, norm_ffn_g, w_query, sub_keys, expert_down, expert_up, final_norm_g):
    assert w_ada.shape[0] == 1, "single-layer problem: the final RMSNorm is fused into the PEER kernel"
    xt = x.reshape(TOKENS, D_MODEL)
    c_pad = jnp.pad(c, ((0, SUBLANES - BATCH), (0, 0)))
    cs_tab, ss_tab = _position_dft_tables()
    csc_tab = _channel_dft_table()
    for layer in range(1):
        mod = _adaln(c_pad, w_ada[layer], b_ada[layer][None, :])[:BATCH]
        shift_m, scale_m, gate_m, shift_f, scale_f, gate_f = [
            m.reshape(BATCH, 1, D_MODEL) for m in jnp.split(mod, 6, axis=-1)]

        w = w_in[layer]
        wf = w[:, :D_FNET].astype(BF16)
        wz = w[:, D_FNET:D_FNET + D_SSM].astype(BF16)
        wx = w[:, D_FNET + D_SSM:D_FNET + D_SSM + D_CONV].astype(BF16)
        wdt = _pad_lanes(w[:, D_FNET + D_SSM + D_CONV:], DT_PAD).astype(BF16)
        uc, us, z, xbc, dt, dtT = _inproj(xt, norm_mix_g[layer][None, :], shift_m, scale_m,
                                          wf, wz, wx, wdt, wdt.T, csc_tab)
        y_fnet = _dft(cs_tab, ss_tab, uc, us)

        xconv = _conv(xbc.reshape(BATCH, SEQ, D_CONV), conv_w[layer], conv_b[layer][None, :])
        alog_row = _pad_lanes(jnp.concatenate([a_log_fwd[layer], a_log_bwd[layer]])[None, :], DT_PAD)
        bias_row = _pad_lanes(jnp.concatenate([dt_bias_fwd[layer], dt_bias_bwd[layer]])[None, :], DT_PAD)
        dskip_row = jnp.repeat(d_skip[layer], SSM_HEAD_DIM)[None, :]
        y_ssm = _ssd(xconv.reshape(TOKENS, D_CONV), dt, dtT, z, alog_row, bias_row, alog_row.T, bias_row.T,
                     dskip_row, ssm_norm_g[layer][None, :])

        wo = w_out[layer].astype(BF16)
        x1, h2, h2T = _outproj(xt, y_fnet, y_ssm, wo[:D_FNET], wo[D_FNET:], gate_m,
                          norm_ffn_g[layer][None, :], shift_f, scale_f)

        keys = sub_keys[layer].reshape(2 * PEER_HEADS, PEER_KEYS, PEER_HALF).astype(BF16)
        c2, e2, n1, e1 = _plan(h2, w_query[layer].astype(BF16), keys)
        xt = _dense(h2T, expert_down[layer].astype(BF16), _transpose_table(expert_up[layer]),
                    c2, e2, n1, e1, x1, gate_f, final_norm_g[None, :])
    return xt.reshape(BATCH, SEQ, D_MODEL)
```

```python
import functools
import math

import jax
import jax.numpy as jnp
from jax import lax
from jax.experimental import pallas as pl
from jax.experimental.pallas import tpu as pltpu

F32 = jnp.float32
BF16 = jnp.bfloat16
HIGHEST = lax.Precision.HIGHEST

D_MODEL = 1024
BATCH = 4
SEQ = 4096
TOKENS = BATCH * SEQ
D_MIX = 2 * D_MODEL
D_FNET = D_MIX // 4
N_FNET_GROUPS = 4
FNET_GROUP = D_FNET // N_FNET_GROUPS
D_SSM = D_MIX - D_FNET
SSM_HEAD_DIM = 64
N_SSM_HEADS = D_SSM // SSM_HEAD_DIM
N_BC_GROUPS = 4
HEADS_PER_GROUP = N_SSM_HEADS // N_BC_GROUPS
D_STATE = 128
CONV_WIDTH = 5
SSD_CHUNK = 128
N_CHUNKS = SEQ // SSD_CHUNK
SSD_STEP_CHUNKS = 4
D_BC = N_BC_GROUPS * D_STATE
D_CONV = D_SSM + 2 * D_BC
GROUP_WIDTH = D_SSM // N_BC_GROUPS
PEER_HEADS = 8
PEER_KEYS = 128
PEER_EXPERTS = PEER_KEYS * PEER_KEYS
PEER_HALF = 128
PEER_TOPK = 16
EPS = 1e-6

LANES = 128
SUBLANES = 8
BF16_ROWS = 16
DT_PAD = LANES
VMEM_LIMIT = 56 * 1024 * 1024

ADA_TN = 512
PROJ_TM = 512
DFT_TM = 512
CONV_TC = 256
CONV_TR = 512
CONV_HALO = SUBLANES
OUT_TM = 512
PLAN_TT = 256
DENSE_TT = 512
DENSE_TE = 2048
DENSE_SUB = 256
DENSE_CHUNK = 256


def _cparams(sem):
    return pltpu.CompilerParams(dimension_semantics=sem, vmem_limit_bytes=VMEM_LIMIT)


def _sigmoid(x):
    return 1.0 / (1.0 + jnp.exp(-x))


def _bf16_terms(x):
    terms = []
    rest = x
    for _ in range(3):
        term = rest.astype(BF16)
        terms.append(term)
        rest = rest - term.astype(F32)
    return terms


def _softplus(x):
    return jnp.maximum(x, 0.0) + jnp.log(1.0 + jnp.exp(-jnp.abs(x)))


def _adaln_kernel(c_ref, w_ref, b_ref, o_ref):
    c = c_ref[...]
    ca = c * _sigmoid(c)
    o_ref[...] = jnp.dot(ca, w_ref[...], preferred_element_type=F32, precision=HIGHEST) + b_ref[...]


def _adaln(c_pad, w_ada, b_ada):
    n = w_ada.shape[1]
    return pl.pallas_call(
        _adaln_kernel,
        grid=(n // ADA_TN,),
        in_specs=[
            pl.BlockSpec((SUBLANES, D_MODEL), lambda j: (0, 0)),
            pl.BlockSpec((D_MODEL, ADA_TN), lambda j: (0, j)),
            pl.BlockSpec((1, ADA_TN), lambda j: (0, j)),
        ],
        out_specs=pl.BlockSpec((SUBLANES, ADA_TN), lambda j: (0, j)),
        out_shape=jax.ShapeDtypeStruct((SUBLANES, n), F32),
        compiler_params=_cparams(("arbitrary",)),
        name="adaln",
    )(c_pad, w_ada, b_ada)


def _modulated_norm(x, gain, shift, scale):
    ms = jnp.mean(x * x, axis=-1, keepdims=True)
    xn = x * lax.rsqrt(ms + EPS) * gain
    return xn * (1.0 + scale) + shift


def _inproj_kernel(x_ref, g_ref, shift_ref, scale_ref, wf_ref, wz_ref, wx_ref, wdt_ref, wdtT_ref, csc_ref,
                   uc_ref, us_ref, z_ref, xbc_ref, dt_ref, dtT_ref):
    h = _modulated_norm(x_ref[...], g_ref[...], shift_ref[0], scale_ref[0])
    hb = h.astype(BF16)
    fb = jnp.dot(hb, wf_ref[...], preferred_element_type=F32).astype(BF16)
    for g in range(N_FNET_GROUPS):
        sl = slice(g * FNET_GROUP, (g + 1) * FNET_GROUP)
        u = jnp.dot(fb[:, sl], csc_ref[...], preferred_element_type=F32)
        uc_ref[:, sl] = u[:, :FNET_GROUP].astype(BF16)
        us_ref[:, sl] = u[:, FNET_GROUP:].astype(BF16)
    z_ref[...] = jnp.dot(hb, wz_ref[...], preferred_element_type=F32).astype(BF16)
    xbc_ref[...] = jnp.dot(hb, wx_ref[...], preferred_element_type=F32).astype(BF16)
    dt_ref[...] = jnp.dot(hb, wdt_ref[...], preferred_element_type=F32)
    dtT_ref[...] = lax.dot_general(wdtT_ref[...], hb, (((1,), (1,)), ((), ())), preferred_element_type=F32)


def _inproj(x2d, gain, shift, scale, wf, wz, wx, wdt, wdtT, csc):
    tiles_per_batch = SEQ // PROJ_TM
    row = lambda i: (i, 0)
    const = lambda i: (0, 0)
    per_batch = lambda i: (i // tiles_per_batch, 0, 0)
    return pl.pallas_call(
        _inproj_kernel,
        grid=(TOKENS // PROJ_TM,),
        in_specs=[
            pl.BlockSpec((PROJ_TM, D_MODEL), row),
            pl.BlockSpec((1, D_MODEL), const),
            pl.BlockSpec((1, 1, D_MODEL), per_batch),
            pl.BlockSpec((1, 1, D_MODEL), per_batch),
            pl.BlockSpec(wf.shape, const),
            pl.BlockSpec(wz.shape, const),
            pl.BlockSpec(wx.shape, const),
            pl.BlockSpec(wdt.shape, const),
            pl.BlockSpec(wdtT.shape, const),
            pl.BlockSpec(csc.shape, const),
        ],
        out_specs=[
            pl.BlockSpec((PROJ_TM, D_FNET), row),
            pl.BlockSpec((PROJ_TM, D_FNET), row),
            pl.BlockSpec((PROJ_TM, D_SSM), row),
            pl.BlockSpec((PROJ_TM, D_CONV), row),
            pl.BlockSpec((PROJ_TM, DT_PAD), row),
            pl.BlockSpec((DT_PAD, PROJ_TM), lambda i: (0, i)),
        ],
        out_shape=[
            jax.ShapeDtypeStruct((TOKENS, D_FNET), BF16),
            jax.ShapeDtypeStruct((TOKENS, D_FNET), BF16),
            jax.ShapeDtypeStruct((TOKENS, D_SSM), BF16),
            jax.ShapeDtypeStruct((TOKENS, D_CONV), BF16),
            jax.ShapeDtypeStruct((TOKENS, DT_PAD), F32),
            jax.ShapeDtypeStruct((DT_PAD, TOKENS), F32),
        ],
        compiler_params=_cparams(("arbitrary",)),
        name="inproj",
    )(x2d, gain, shift, scale, wf, wz, wx, wdt, wdtT, csc)


def _dft_kernel(cs_ref, ss_ref, uc_ref, us_ref, o_ref):
    y = jnp.dot(cs_ref[...], uc_ref[...], preferred_element_type=F32)
    y = y - jnp.dot(ss_ref[...], us_ref[...], preferred_element_type=F32)
    o_ref[...] = y.astype(BF16)


def _dft(cs, ss, uc, us):
    mt = SEQ // DFT_TM
    return pl.pallas_call(
        _dft_kernel,
        grid=(BATCH, mt),
        in_specs=[
            pl.BlockSpec((DFT_TM, SEQ), lambda b, m: (m, 0)),
            pl.BlockSpec((DFT_TM, SEQ), lambda b, m: (m, 0)),
            pl.BlockSpec((SEQ, D_FNET), lambda b, m: (b, 0)),
            pl.BlockSpec((SEQ, D_FNET), lambda b, m: (b, 0)),
        ],
        out_specs=pl.BlockSpec((DFT_TM, D_FNET), lambda b, m: (b * mt + m, 0)),
        out_shape=jax.ShapeDtypeStruct((TOKENS, D_FNET), BF16),
        compiler_params=_cparams(("arbitrary", "arbitrary")),
        name="fnet_dft",
    )(cs, ss, uc, us)


def _conv_kernel(x_ref, w_ref, b_ref, o_ref, pad_ref):
    zeros = jnp.zeros((CONV_HALO, CONV_TC), F32)
    pad_ref[0:CONV_HALO, :] = zeros
    pad_ref[SEQ + CONV_HALO:SEQ + 2 * CONV_HALO, :] = zeros
    pad_ref[CONV_HALO:SEQ + CONV_HALO, :] = x_ref[0].astype(F32)
    w = w_ref[...]
    bias = b_ref[...]
    first = CONV_HALO - CONV_WIDTH // 2
    for i in range(SEQ // CONV_TR):
        acc = jnp.broadcast_to(bias, (CONV_TR, CONV_TC))
        for k in range(CONV_WIDTH):
            lo = first + k + i * CONV_TR
            acc = acc + pad_ref[lo:lo + CONV_TR, :] * w[k:k + 1, :]
        o_ref[0, i * CONV_TR:(i + 1) * CONV_TR, :] = (acc * _sigmoid(acc)).astype(BF16)


def _conv(xbc3, conv_w, conv_b):
    return pl.pallas_call(
        _conv_kernel,
        grid=(BATCH, D_CONV // CONV_TC),
        in_specs=[
            pl.BlockSpec((1, SEQ, CONV_TC), lambda b, j: (b, 0, j)),
            pl.BlockSpec((CONV_WIDTH, CONV_TC), lambda b, j: (0, j)),
            pl.BlockSpec((1, CONV_TC), lambda b, j: (0, j)),
        ],
        out_specs=pl.BlockSpec((1, SEQ, CONV_TC), lambda b, j: (b, 0, j)),
        out_shape=jax.ShapeDtypeStruct((BATCH, SEQ, D_CONV), BF16),
        scratch_shapes=[pltpu.VMEM((SEQ + 2 * CONV_HALO, CONV_TC), F32)],
        compiler_params=_cparams(("arbitrary", "arbitrary")),
        name="conv_silu",
    )(xbc3, conv_w, conv_b)


def _ssd_direction(xc, dtvT, cs, csT, a, aT, decay_row, state_ref, yacc_ref, head_base, forward):
    li = lax.broadcasted_iota(jnp.int32, (SSD_CHUNK, SSD_CHUNK), 0)
    si = lax.broadcasted_iota(jnp.int32, (SSD_CHUNK, SSD_CHUNK), 1)
    low_half = si < SSM_HEAD_DIM
    if forward:
        u, uT = cs, csT
        mask = li >= si
    else:
        u, uT = cs - a, csT - aT
        mask = si >= li
    pair = 2 * SSM_HEAD_DIM
    for g in range(N_BC_GROUPS):
        bc = xc[:, D_SSM + g * D_STATE:D_SSM + (g + 1) * D_STATE]
        cc = xc[:, D_SSM + D_BC + g * D_STATE:D_SSM + D_BC + (g + 1) * D_STATE]
        gmat = lax.dot_general(cc, bc, (((1,), (1,)), ((), ())), preferred_element_type=F32)
        bcT = bc.astype(F32).T
        ccf = cc.astype(F32)
        for pr in range(HEADS_PER_GROUP // 2):
            h0 = g * HEADS_PER_GROUP + 2 * pr
            xcols = slice(h0 * SSM_HEAD_DIM, h0 * SSM_HEAD_DIM + pair)
            scols = slice(pr * pair, (pr + 1) * pair)
            xpair = xc[:, xcols]
            spair = state_ref[g, :, scols]
            rhs = jnp.concatenate([xpair, spair.astype(BF16)], axis=0)
            ys, upds = [], []
            for j in range(2):
                hh = head_base + h0 + j
                ub = jnp.broadcast_to(u[:, hh:hh + 1], (SSD_CHUNK, SSD_CHUNK))
                urow = uT[hh:hh + 1, :]
                dtrow = dtvT[hh:hh + 1, :]
                tot = cs[SSD_CHUNK - 1:SSD_CHUNK, hh:hh + 1]
                if forward:
                    diff = ub - urow
                    cscale = jnp.exp(ub)
                    wrow = jnp.exp(tot - urow) * dtrow
                else:
                    diff = urow - ub
                    cscale = jnp.exp(tot - ub)
                    wrow = jnp.exp(urow) * dtrow
                lmat = jnp.where(mask, jnp.exp(diff), 0.0)
                lhs = jnp.concatenate([(gmat * lmat * dtrow).astype(BF16), (ccf * cscale).astype(BF16)], axis=1)
                ys.append(jnp.dot(lhs, rhs, preferred_element_type=F32))
                upds.append(jnp.dot((bcT * wrow).astype(BF16), xpair, preferred_element_type=F32))
            yacc_ref[:, xcols] = jnp.where(low_half, ys[0], ys[1])
            state_ref[g, :, scols] = spair * decay_row[:, xcols] + jnp.where(low_half, upds[0], upds[1])


def _ssd_kernel(xc_ref, dt_ref, dtT_ref, z_ref, alog_row_ref, bias_row_ref, alog_col_ref, bias_col_ref,
                dskip_ref, ng_ref, expand_ref, y_ref, state_ref, yacc_ref, yb_ref):
    phase = pl.program_id(1)
    step = pl.program_id(2)

    @pl.when(step == 0)
    def _():
        state_ref[...] = jnp.zeros(state_ref.shape, F32)

    li = lax.broadcasted_iota(jnp.int32, (SSD_CHUNK, SSD_CHUNK), 0)
    si = lax.broadcasted_iota(jnp.int32, (SSD_CHUNK, SSD_CHUNK), 1)
    lower = (li >= si).astype(BF16)
    upper = (li <= si).astype(BF16)
    used = 2 * N_SSM_HEADS

    def scan_chunk(sub, forward):
        rows = slice(sub * SSD_CHUNK, (sub + 1) * SSD_CHUNK)
        xc = xc_ref[rows, :]
        dtv = _softplus(dt_ref[rows, :] + bias_row_ref[...])
        a = dtv * (-jnp.exp(alog_row_ref[...]))
        dtvT = _softplus(dtT_ref[0:used, rows] + bias_col_ref[0:used, :])
        aT = dtvT * (-jnp.exp(alog_col_ref[0:used, :]))
        cs = sum(jnp.dot(lower, t, preferred_element_type=F32) for t in _bf16_terms(a))
        csT = sum(jnp.dot(t, upper, preferred_element_type=F32) for t in _bf16_terms(aT))
        chunk_decay = jnp.exp(cs[SSD_CHUNK - 1:SSD_CHUNK, :])
        expand = expand_ref[0 if forward else 1]
        decay_row = sum(jnp.dot(t, expand, preferred_element_type=F32) for t in _bf16_terms(chunk_decay))
        _ssd_direction(xc, dtvT, cs, csT, a, aT, decay_row, state_ref, yacc_ref,
                       0 if forward else N_SSM_HEADS, forward)
        return rows, xc

    @pl.when(phase == 0)
    def _():
        for sub in reversed(range(SSD_STEP_CHUNKS)):
            scan_chunk(sub, False)
            chunk = (N_CHUNKS // SSD_STEP_CHUNKS - 1 - step) * SSD_STEP_CHUNKS + sub
            yb_ref[chunk] = yacc_ref[...].astype(BF16)

    @pl.when(phase == 1)
    def _():
        for sub in range(SSD_STEP_CHUNKS):
            rows, xc = scan_chunk(sub, True)
            xs = xc[:, :D_SSM].astype(F32)
            y = yacc_ref[...] + yb_ref[step * SSD_STEP_CHUNKS + sub].astype(F32) + xs * dskip_ref[...]
            z = z_ref[rows, :].astype(F32)
            y = y * (z * _sigmoid(z))
            for g in range(N_BC_GROUPS):
                gs = slice(g * GROUP_WIDTH, (g + 1) * GROUP_WIDTH)
                yg = y[:, gs]
                ms = jnp.mean(yg * yg, axis=-1, keepdims=True)
                y_ref[rows, gs] = (yg * lax.rsqrt(ms + EPS) * ng_ref[:, gs]).astype(BF16)


def _head_expansion():
    col = lax.broadcasted_iota(jnp.int32, (2, DT_PAD, D_SSM), 1)
    head = lax.broadcasted_iota(jnp.int32, (2, DT_PAD, D_SSM), 2) // SSM_HEAD_DIM
    direction = lax.broadcasted_iota(jnp.int32, (2, DT_PAD, D_SSM), 0)
    return (col == direction * N_SSM_HEADS + head).astype(BF16)


def _ssd(xconv, dt, dtT, z, alog_row, bias_row, alog_col, bias_col, dskip_row, norm_g):
    steps = N_CHUNKS // SSD_STEP_CHUNKS
    step_rows = SSD_STEP_CHUNKS * SSD_CHUNK

    def block_of(p, s):
        return p * s + (1 - p) * (steps - 1 - s)

    rows = lambda b, p, s: (b * steps + block_of(p, s), 0)
    cols = lambda b, p, s: (0, b * steps + block_of(p, s))
    fwd_only = lambda b, p, s: (b * steps + p * s, 0)
    const = lambda b, p, s: (0, 0)
    return pl.pallas_call(
        _ssd_kernel,
        grid=(BATCH, 2, steps),
        in_specs=[
            pl.BlockSpec((step_rows, D_CONV), rows),
            pl.BlockSpec((step_rows, DT_PAD), rows),
            pl.BlockSpec((DT_PAD, step_rows), cols),
            pl.BlockSpec((step_rows, D_SSM), fwd_only),
            pl.BlockSpec((1, DT_PAD), const),
            pl.BlockSpec((1, DT_PAD), const),
            pl.BlockSpec((DT_PAD, 1), const),
            pl.BlockSpec((DT_PAD, 1), const),
            pl.BlockSpec((1, D_SSM), const),
            pl.BlockSpec((1, D_SSM), const),
            pl.BlockSpec((2, DT_PAD, D_SSM), lambda b, p, s: (0, 0, 0)),
        ],
        out_specs=pl.BlockSpec((step_rows, D_SSM), fwd_only),
        out_shape=jax.ShapeDtypeStruct((TOKENS, D_SSM), BF16),
        scratch_shapes=[
            pltpu.VMEM((N_BC_GROUPS, D_STATE, GROUP_WIDTH), F32),
            pltpu.VMEM((SSD_CHUNK, D_SSM), F32),
            pltpu.VMEM((N_CHUNKS, SSD_CHUNK, D_SSM), BF16),
        ],
        compiler_params=_cparams(("arbitrary", "arbitrary", "arbitrary")),
        name="ssd_scan",
    )(xconv, dt, dtT, z, alog_row, bias_row, alog_col, bias_col, dskip_row, norm_g, _head_expansion())


def _outproj_kernel(x_ref, yf_ref, ys_ref, wf_ref, ws_ref, gate_ref, g_ref, shift_ref, scale_ref,
                    x1_ref, h2_ref, h2T_ref):
    mix = jnp.dot(yf_ref[...], wf_ref[...], preferred_element_type=F32)
    mix = mix + jnp.dot(ys_ref[...], ws_ref[...], preferred_element_type=F32)
    x1 = x_ref[...] + gate_ref[0] * mix
    x1_ref[...] = x1
    h2 = _modulated_norm(x1, g_ref[...], shift_ref[0], scale_ref[0])
    h2_ref[...] = h2.astype(BF16)
    h2T_ref[...] = h2.T.astype(BF16)


def _outproj(x2d, yf, ys, wf, ws, gate, gain, shift, scale):
    tiles_per_batch = SEQ // OUT_TM
    row = lambda i: (i, 0)
    const = lambda i: (0, 0)
    per_batch = lambda i: (i // tiles_per_batch, 0, 0)
    return pl.pallas_call(
        _outproj_kernel,
        grid=(TOKENS // OUT_TM,),
        in_specs=[
            pl.BlockSpec((OUT_TM, D_MODEL), row),
            pl.BlockSpec((OUT_TM, D_FNET), row),
            pl.BlockSpec((OUT_TM, D_SSM), row),
            pl.BlockSpec(wf.shape, const),
            pl.BlockSpec(ws.shape, const),
            pl.BlockSpec((1, 1, D_MODEL), per_batch),
            pl.BlockSpec((1, D_MODEL), const),
            pl.BlockSpec((1, 1, D_MODEL), per_batch),
            pl.BlockSpec((1, 1, D_MODEL), per_batch),
        ],
        out_specs=[pl.BlockSpec((OUT_TM, D_MODEL), row), pl.BlockSpec((OUT_TM, D_MODEL), row),
                   pl.BlockSpec((D_MODEL, OUT_TM), lambda i: (0, i))],
        out_shape=[jax.ShapeDtypeStruct((TOKENS, D_MODEL), F32), jax.ShapeDtypeStruct((TOKENS, D_MODEL), BF16),
                   jax.ShapeDtypeStruct((D_MODEL, TOKENS), BF16)],
        compiler_params=_cparams(("arbitrary",)),
        name="outproj",
    )(x2d, yf, ys, wf, ws, gate, gain, shift, scale)


MARKER_UNIT = 2.0 ** 100


def _extract_topk_marked(x, k):
    work = x
    vals = []
    for r in range(k):
        m = jnp.max(work, axis=0, keepdims=True)
        work = jnp.where(work == m, -(k + r) * MARKER_UNIT, work)
        vals.append(m)
    marked = work <= -k * MARKER_UNIT
    rank = jnp.where(marked, work * (-1.0 / MARKER_UNIT) - float(k), float(k))
    in_range = jnp.min(x, axis=0, keepdims=True) > -MARKER_UNIT
    return jnp.concatenate(vals, axis=0), rank, in_range


def _extract_topk(x, k, break_ties):
    rows = x.shape[0]
    iota = lax.broadcasted_iota(jnp.int32, x.shape, 0).astype(F32)
    rank = jnp.full(x.shape, float(k), F32)
    work = x
    vals = []
    for r in range(k):
        m = jnp.max(work, axis=0, keepdims=True)
        sel = work == m
        if break_ties:
            first = jnp.min(jnp.where(sel, iota, float(rows)), axis=0, keepdims=True)
            sel = iota == first
        rank = jnp.where(sel, float(r), rank)
        work = jnp.where(sel, -jnp.inf, work)
        vals.append(m)
    return jnp.concatenate(vals, axis=0), rank


def _candidate_width(r):
    return PEER_TOPK // (r + 1)


def _candidates(v1, v2, pad):
    sub = lax.broadcasted_iota(jnp.int32, (SUBLANES, v1.shape[1]), 0)
    blocks = [v1[0:1] + v2, v1[1:2] + v2[0:SUBLANES]]
    for r in range(2, SUBLANES):
        blocks.append(jnp.where(sub < _candidate_width(r), v1[r:r + 1] + v2[0:SUBLANES], pad))
    blocks.append(v1[SUBLANES:] + v2[0:1])
    return jnp.concatenate(blocks, axis=0)


def _row_counts(chosen):
    counts = [jnp.sum(chosen[0:PEER_TOPK], axis=0, keepdims=True)]
    for r in range(1, SUBLANES):
        lo = PEER_TOPK + (r - 1) * SUBLANES
        counts.append(jnp.sum(chosen[lo:lo + SUBLANES], axis=0, keepdims=True))
    tail = chosen[PEER_TOPK + (SUBLANES - 1) * SUBLANES:]
    counts.extend(tail[j:j + 1] for j in range(SUBLANES))
    return counts


def _plan_head(s1, s2, break_ties):
    if break_ties:
        v1, r1 = _extract_topk(s1, PEER_TOPK, True)
        v2, r2 = _extract_topk(s2, PEER_TOPK, True)
        in_range = None
    else:
        v1, r1, ok1 = _extract_topk_marked(s1, PEER_TOPK)
        v2, r2, ok2 = _extract_topk_marked(s2, PEER_TOPK)
        in_range = jnp.logical_and(ok1, ok2)
    if break_ties:
        cand = _candidates(v1, v2, -jnp.inf)
        _, crank = _extract_topk(cand, PEER_TOPK, True)
    else:
        cand = _candidates(v1, v2, -4.0 * MARKER_UNIT)
        _, crank, _ = _extract_topk_marked(cand, PEER_TOPK)
    chosen = (crank < float(PEER_TOPK)).astype(F32)
    counts = _row_counts(chosen)
    top = v1[0:1] + v2[0:1]
    zsum = jnp.sum(chosen * jnp.exp(cand - top), axis=0, keepdims=True)
    r1b = r1.astype(BF16)
    n1 = jnp.zeros(s1.shape, BF16)
    for r in range(PEER_TOPK):
        n1 = jnp.where(r1b == float(r), counts[r].astype(BF16), n1)
    e1 = jnp.exp(s1 - v1[0:1]) / zsum
    e2 = jnp.exp(s2 - v2[0:1])
    ranked = (r1 < float(PEER_TOPK)).astype(F32) + (r2 < float(PEER_TOPK)).astype(F32)
    total = jnp.sum(ranked, axis=0, keepdims=True) + jnp.sum(chosen, axis=0, keepdims=True)
    good = total == 3.0 * PEER_TOPK
    if in_range is not None:
        good = jnp.logical_and(good, in_range)
    clean = jnp.min(jnp.where(good, 1.0, 0.0)) > 0.5
    return r2, e2, n1.astype(F32), e1, clean


def _twin_bf16_words(x):
    hi = pltpu.bitcast(x.astype(BF16).astype(F32), jnp.uint32)
    return hi | (hi >> 16)


def _plan_kernel(h2_ref, wq_ref, keys_ref, c2_ref, e2_ref, n1_ref, e1_ref, q_ref):
    q_ref[...] = jnp.dot(h2_ref[...], wq_ref[...], preferred_element_type=F32).astype(BF16)

    def per_head(h, carry):
        def scores(side):
            col = pl.multiple_of((2 * h + side) * PEER_HALF, PEER_HALF)
            qh = q_ref[:, pl.ds(col, PEER_HALF)]
            return lax.dot_general(keys_ref[2 * h + side], qh, (((1,), (1,)), ((), ())), preferred_element_type=F32)

        s1 = scores(0)
        s2 = scores(1)

        def store(r2, e2, n1, e1):
            c2_ref[h] = r2.astype(BF16)
            e2_ref[h] = e2.astype(BF16)
            for c in range(PLAN_TT // LANES):
                cols = slice(c * LANES, (c + 1) * LANES)
                n1_ref[h, c] = _twin_bf16_words(n1[:, cols])
                e1_ref[h, c] = _twin_bf16_words(e1[:, cols])

        r2, e2, n1, e1, clean = _plan_head(s1, s2, break_ties=False)
        store(r2, e2, n1, e1)

        @pl.when(jnp.logical_not(clean))
        def _():
            store(*_plan_head(s1, s2, break_ties=True)[:4])

        return carry

    lax.fori_loop(0, PEER_HEADS, per_head, 0)


def _plan(h2, wq, keys):
    blk = lambda i: (0, 0, i)
    shp = (PEER_HEADS, PEER_KEYS, TOKENS)
    lane_blk = lambda i: (0, i, 0, 0)
    lane_shp = (PEER_HEADS, TOKENS // LANES, PEER_KEYS, LANES)
    lane_spec = pl.BlockSpec((PEER_HEADS, PLAN_TT // LANES, PEER_KEYS, LANES), lane_blk)
    return pl.pallas_call(
        _plan_kernel,
        grid=(TOKENS // PLAN_TT,),
        in_specs=[
            pl.BlockSpec((PLAN_TT, D_MODEL), lambda i: (i, 0)),
            pl.BlockSpec(wq.shape, lambda i: (0, 0)),
            pl.BlockSpec(keys.shape, lambda i: (0, 0, 0)),
        ],
        out_specs=[pl.BlockSpec((PEER_HEADS, PEER_KEYS, PLAN_TT), blk)] * 2 + [lane_spec] * 2,
        out_shape=[
            jax.ShapeDtypeStruct(shp, BF16),
            jax.ShapeDtypeStruct(shp, BF16),
            jax.ShapeDtypeStruct(lane_shp, jnp.uint32),
            jax.ShapeDtypeStruct(lane_shp, jnp.uint32),
        ],
        scratch_shapes=[pltpu.VMEM((PLAN_TT, 2 * PEER_HEADS * PEER_HALF), BF16)],
        compiler_params=_cparams(("arbitrary",)),
        name="peer_plan",
    )(h2, wq, keys)


def _gelu_bf16(x):
    half_x = x.astype(BF16) * 0.5
    return half_x + half_x * lax.erf(x * (1.0 / math.sqrt(2.0))).astype(BF16)


DENSE_NE = PEER_EXPERTS // DENSE_TE
DENSE_NT = TOKENS // DENSE_TT
DENSE_TILES = DENSE_NT * DENSE_NE
DENSE_KEYS_PER_TILE = DENSE_TE // PEER_KEYS


def _dense_stage_body(f, h2_ref, down_ref, upT_ref, c2_ref, e2_ref, n1_ref, e1_ref, acc_ref, act_w, act_r):
    gate_tile = jnp.clip(f - 1, 0, DENSE_TILES - 1) % DENSE_NE

    def act_piece(q, c):
        rows = slice(q * DENSE_SUB, (q + 1) * DENSE_SUB)
        cols = slice(c * DENSE_CHUNK, (c + 1) * DENSE_CHUNK)
        act_w[rows, cols] = jnp.dot(down_ref[rows, :], h2_ref[:, cols], preferred_element_type=F32)

    def gate_piece(k, c):
        i1 = gate_tile * DENSE_KEYS_PER_TILE + k
        cols = slice(c * DENSE_CHUNK, (c + 1) * DENSE_CHUNK)
        rows = slice(k * PEER_KEYS, (k + 1) * PEER_KEYS)

        def row_tile(ref, h):
            tiles = []
            for t in range(DENSE_CHUNK // LANES):
                lane_tile = c * (DENSE_CHUNK // LANES) + t
                words = jnp.broadcast_to(ref[h, lane_tile, pl.ds(i1, 1), :], (SUBLANES, LANES))
                tiles.append(jnp.concatenate([pltpu.bitcast(words, BF16)] * (PEER_KEYS // BF16_ROWS), axis=0))
            return jnp.concatenate(tiles, axis=1)

        gates = jnp.zeros((PEER_KEYS, DENSE_CHUNK), BF16)
        for h in range(PEER_HEADS):
            hit = c2_ref[h, :, cols] < row_tile(n1_ref, h)
            gates = gates + jnp.where(hit, e2_ref[h, :, cols], jnp.zeros((), BF16)) * row_tile(e1_ref, h)
        return gates * _gelu_bf16(act_r[rows, cols])

    keys_per_sub = DENSE_SUB // PEER_KEYS
    n_sub = DENSE_TE // DENSE_SUB
    for c in range(DENSE_TT // DENSE_CHUNK):
        cols = slice(c * DENSE_CHUNK, (c + 1) * DENSE_CHUNK)
        part = None
        for j in range(n_sub):
            w = jnp.concatenate([gate_piece(keys_per_sub * j + k, c) for k in range(keys_per_sub)], axis=0)
            d = jnp.dot(upT_ref[:, j * DENSE_SUB:(j + 1) * DENSE_SUB], w, preferred_element_type=F32)
            part = d if part is None else part + d
            act_piece(j, c)
        acc_ref[:, cols] += part


def _dense_kernel(h2_ref, down_ref, upT_ref, c2_ref, e2_ref, n1_ref, e1_ref, x1_ref, gate_ref, g_ref,
                  o_ref, acc_ref, act0_ref, act1_ref):
    f = pl.program_id(0)
    out_tile = jnp.clip(f - 1, 0, DENSE_TILES - 1) % DENSE_NE

    @pl.when(f == 0)
    def _():
        act1_ref[...] = jnp.zeros(act1_ref.shape, F32)

    @pl.when(out_tile == 0)
    def _():
        acc_ref[...] = jnp.zeros(acc_ref.shape, F32)

    stage = functools.partial(_dense_stage_body, f, h2_ref, down_ref, upT_ref, c2_ref, e2_ref, n1_ref, e1_ref,
                              acc_ref)

    @pl.when(f % 2 == 0)
    def _():
        stage(act0_ref, act1_ref)

    @pl.when(f % 2 == 1)
    def _():
        stage(act1_ref, act0_ref)

    @pl.when(jnp.logical_and(out_tile == DENSE_NE - 1, f >= 1))
    def _():
        x2 = x1_ref[...] + gate_ref[0] * acc_ref[...].T
        ms = jnp.mean(x2 * x2, axis=-1, keepdims=True)
        o_ref[...] = x2 * lax.rsqrt(ms + EPS) * g_ref[...]


def _dense(h2T, down, upT, c2, e2, n1, e1, x1, gate, final_g):
    tiles_per_batch = SEQ // DENSE_TT

    def tile(f, lag):
        return jnp.clip(f - lag, 0, DENSE_TILES - 1)

    plan = lambda f: (0, tile(f, 1) // DENSE_NE, 0, 0)
    out_tok = lambda f: (tile(f, 1) // DENSE_NE, 0)
    plan_blk = (PEER_HEADS, DENSE_TT // LANES, PEER_KEYS, LANES)
    rank_map = lambda f: (0, 0, tile(f, 1) // DENSE_NE)
    rank_blk = (PEER_HEADS, PEER_KEYS, DENSE_TT)
    return pl.pallas_call(
        _dense_kernel,
        grid=(DENSE_TILES + 1,),
        in_specs=[
            pl.BlockSpec((D_MODEL, DENSE_TT), lambda f: (0, tile(f, 0) // DENSE_NE)),
            pl.BlockSpec((DENSE_TE, D_MODEL), lambda f: (tile(f, 0) % DENSE_NE, 0)),
            pl.BlockSpec((D_MODEL, DENSE_TE), lambda f: (0, tile(f, 1) % DENSE_NE)),
            pl.BlockSpec(rank_blk, rank_map),
            pl.BlockSpec(rank_blk, rank_map),
            pl.BlockSpec(plan_blk, plan),
            pl.BlockSpec(plan_blk, plan),
            pl.BlockSpec((DENSE_TT, D_MODEL), out_tok),
            pl.BlockSpec((1, 1, D_MODEL), lambda f: (tile(f, 1) // DENSE_NE // tiles_per_batch, 0, 0)),
            pl.BlockSpec((1, D_MODEL), lambda f: (0, 0)),
        ],
        out_specs=pl.BlockSpec((DENSE_TT, D_MODEL), out_tok),
        out_shape=jax.ShapeDtypeStruct((TOKENS, D_MODEL), F32),
        scratch_shapes=[
            pltpu.VMEM((D_MODEL, DENSE_TT), F32),
            pltpu.VMEM((DENSE_TE, DENSE_TT), F32),
            pltpu.VMEM((DENSE_TE, DENSE_TT), F32),
        ],
        compiler_params=_cparams(("arbitrary",)),
        name="peer_dense",
    )(h2T, down, upT, c2, e2, n1, e1, x1, gate, final_g)


def _transpose_kernel(x_ref, o_ref):
    o_ref[...] = x_ref[...].T.astype(BF16)


def _transpose_table(t):
    rows, cols = t.shape
    tr = 512
    return pl.pallas_call(
        _transpose_kernel,
        grid=(rows // tr,),
        in_specs=[pl.BlockSpec((tr, cols), lambda i: (i, 0))],
        out_specs=pl.BlockSpec((cols, tr), lambda i: (0, i)),
        out_shape=jax.ShapeDtypeStruct((cols, rows), BF16),
        compiler_params=_cparams(("arbitrary",)),
        name="table_transpose",
    )(t)


def _position_dft_tables():
    radix = 64
    r = lax.broadcasted_iota(jnp.int32, (radix, SEQ), 0)
    k = lax.broadcasted_iota(jnp.int32, (radix, SEQ), 1)
    coarse = ((r * k) % radix).astype(F32) * (2.0 * math.pi / radix)
    fine = ((r * k) % SEQ).astype(F32) * (2.0 * math.pi / SEQ)
    scale = 1.0 / math.sqrt(SEQ)
    ca, sa = jnp.cos(coarse)[:, None, :] * scale, jnp.sin(coarse)[:, None, :] * scale
    cb, sb = jnp.cos(fine)[None, :, :], jnp.sin(fine)[None, :, :]
    cos_tab = (ca * cb - sa * sb).reshape(SEQ, SEQ).astype(BF16)
    sin_tab = (sa * cb + ca * sb).reshape(SEQ, SEQ).astype(BF16)
    return cos_tab, sin_tab


def _channel_dft_table():
    j = lax.broadcasted_iota(jnp.int32, (FNET_GROUP, FNET_GROUP), 0)
    k = lax.broadcasted_iota(jnp.int32, (FNET_GROUP, FNET_GROUP), 1)
    ang = ((j * k) % FNET_GROUP).astype(F32) * (2.0 * math.pi / FNET_GROUP)
    scale = 1.0 / math.sqrt(FNET_GROUP)
    return jnp.concatenate([jnp.cos(ang) * scale, jnp.sin(ang) * scale], axis=1).astype(BF16)


def _pad_lanes(v, width):
    return jnp.pad(v, ((0, 0), (0, width - v.shape[1])))


def kernel(x, c, w_ada, b_ada, norm_mix_g, w_in, conv_w, conv_b, a_log_fwd, a_log_bwd, dt_bias_fwd, dt_bias_bwd, d_skip, ssm_norm_g, w_out, norm_ffn_g, w_query, sub_keys, expert_down, expert_up, final_norm_g):
    assert w_ada.shape[0] == 1, "single-layer problem: the final RMSNorm is fused into the PEER kernel"
    xt = x.reshape(TOKENS, D_MODEL)
    c_pad = jnp.pad(c, ((0, SUBLANES - BATCH), (0, 0)))
    cs_tab, ss_tab = _position_dft_tables()
    csc_tab = _channel_dft_table()
    for layer in range(1):
        mod = _adaln(c_pad, w_ada[layer], b_ada[layer][None, :])[:BATCH]
        shift_m, scale_m, gate_m, shift_f, scale_f, gate_f = [
            m.reshape(BATCH, 1, D_MODEL) for m in jnp.split(mod, 6, axis=-1)]

        w = w_in[layer]
        wf = w[:, :D_FNET].astype(BF16)
        wz = w[:, D_FNET:D_FNET + D_SSM].astype(BF16)
        wx = w[:, D_FNET + D_SSM:D_FNET + D_SSM + D_CONV].astype(BF16)
        wdt = _pad_lanes(w[:, D_FNET + D_SSM + D_CONV:], DT_PAD).astype(BF16)
        uc, us, z, xbc, dt, dtT = _inproj(xt, norm_mix_g[layer][None, :], shift_m, scale_m,
                                          wf, wz, wx, wdt, wdt.T, csc_tab)
        y_fnet = _dft(cs_tab, ss_tab, uc, us)

        xconv = _conv(xbc.reshape(BATCH, SEQ, D_CONV), conv_w[layer], conv_b[layer][None, :])
        alog_row = _pad_lanes(jnp.concatenate([a_log_fwd[layer], a_log_bwd[layer]])[None, :], DT_PAD)
        bias_row = _pad_lanes(jnp.concatenate([dt_bias_fwd[layer], dt_bias_bwd[layer]])[None, :], DT_PAD)
        dskip_row = jnp.repeat(d_skip[layer], SSM_HEAD_DIM)[None, :]
        y_ssm = _ssd(xconv.reshape(TOKENS, D_CONV), dt, dtT, z, alog_row, bias_row, alog_row.T, bias_row.T,
                     dskip_row, ssm_norm_g[layer][None, :])

        wo = w_out[layer].astype(BF16)
        x1, h2, h2T = _outproj(xt, y_fnet, y_ssm, wo[:D_FNET], wo[D_FNET:], gate_m,
                          norm_ffn_g[layer][None, :], shift_f, scale_f)

        keys = sub_keys[layer].reshape(2 * PEER_HEADS, PEER_KEYS, PEER_HALF).astype(BF16)
        c2, e2, n1, e1 = _plan(h2, w_query[layer].astype(BF16), keys)
        xt = _dense(h2T, expert_down[layer].astype(BF16), _transpose_table(expert_up[layer]),
                    c2, e2, n1, e1, x1, gate_f, final_norm_g[None, :])
    return xt.reshape(BATCH, SEQ, D_MODEL)
```

```python
import functools
import math

import jax
import jax.numpy as jnp
from jax import lax
from jax.experimental import pallas as pl
from jax.experimental.pallas import tpu as pltpu

F32 = jnp.float32
BF16 = jnp.bfloat16
HIGHEST = lax.Precision.HIGHEST

D_MODEL = 1024
BATCH = 4
SEQ = 4096
TOKENS = BATCH * SEQ
D_MIX = 2 * D_MODEL
D_FNET = D_MIX // 4
N_FNET_GROUPS = 4
FNET_GROUP = D_FNET // N_FNET_GROUPS
D_SSM = D_MIX - D_FNET
SSM_HEAD_DIM = 64
N_SSM_HEADS = D_SSM // SSM_HEAD_DIM
N_BC_GROUPS = 4
HEADS_PER_GROUP = N_SSM_HEADS // N_BC_GROUPS
D_STATE = 128
CONV_WIDTH = 5
SSD_CHUNK = 128
N_CHUNKS = SEQ // SSD_CHUNK
SSD_STEP_CHUNKS = 4
D_BC = N_BC_GROUPS * D_STATE
D_CONV = D_SSM + 2 * D_BC
GROUP_WIDTH = D_SSM // N_BC_GROUPS
PEER_HEADS = 8
PEER_KEYS = 128
PEER_EXPERTS = PEER_KEYS * PEER_KEYS
PEER_HALF = 128
PEER_TOPK = 16
EPS = 1e-6

LANES = 128
SUBLANES = 8
BF16_ROWS = 16
DT_PAD = LANES
VMEM_LIMIT = 56 * 1024 * 1024

ADA_TN = 512
PROJ_TM = 512
DFT_TM = 1024
CONV_TC = 256
CONV_TR = 512
CONV_HALO = SUBLANES
OUT_TM = 512
PLAN_TT = 512
DENSE_TT = 512
DENSE_TE = 2048
DENSE_SUB = 256
DENSE_CHUNK = 256


def _cparams(sem):
    return pltpu.CompilerParams(dimension_semantics=sem, vmem_limit_bytes=VMEM_LIMIT)


def _sigmoid(x):
    return 1.0 / (1.0 + jnp.exp(-x))


def _bf16_terms(x):
    terms = []
    rest = x
    for _ in range(3):
        term = rest.astype(BF16)
        terms.append(term)
        rest = rest - term.astype(F32)
    return terms


def _softplus(x):
    return jnp.maximum(x, 0.0) + jnp.log(1.0 + jnp.exp(-jnp.abs(x)))


def _adaln_kernel(c_ref, w_ref, b_ref, o_ref):
    c = c_ref[...]
    ca = c * _sigmoid(c)
    o_ref[...] = jnp.dot(ca, w_ref[...], preferred_element_type=F32, precision=HIGHEST) + b_ref[...]


def _adaln(c_pad, w_ada, b_ada):
    n = w_ada.shape[1]
    return pl.pallas_call(
        _adaln_kernel,
        grid=(n // ADA_TN,),
        in_specs=[
            pl.BlockSpec((SUBLANES, D_MODEL), lambda j: (0, 0)),
            pl.BlockSpec((D_MODEL, ADA_TN), lambda j: (0, j)),
            pl.BlockSpec((1, ADA_TN), lambda j: (0, j)),
        ],
        out_specs=pl.BlockSpec((SUBLANES, ADA_TN), lambda j: (0, j)),
        out_shape=jax.ShapeDtypeStruct((SUBLANES, n), F32),
        compiler_params=_cparams(("arbitrary",)),
        name="adaln",
    )(c_pad, w_ada, b_ada)


def _modulated_norm(x, gain, shift, scale):
    ms = jnp.mean(x * x, axis=-1, keepdims=True)
    xn = x * lax.rsqrt(ms + EPS) * gain
    return xn * (1.0 + scale) + shift


def _inproj_kernel(x_ref, g_ref, shift_ref, scale_ref, wf_ref, wz_ref, wx_ref, wdt_ref, wdtT_ref, csc_ref,
                   uc_ref, us_ref, z_ref, xbc_ref, dt_ref, dtT_ref):
    h = _modulated_norm(x_ref[...], g_ref[...], shift_ref[0], scale_ref[0])
    hb = h.astype(BF16)
    fb = jnp.dot(hb, wf_ref[...], preferred_element_type=F32).astype(BF16)
    for g in range(N_FNET_GROUPS):
        sl = slice(g * FNET_GROUP, (g + 1) * FNET_GROUP)
        u = jnp.dot(fb[:, sl], csc_ref[...], preferred_element_type=F32)
        uc_ref[:, sl] = u[:, :FNET_GROUP].astype(BF16)
        us_ref[:, sl] = u[:, FNET_GROUP:].astype(BF16)
    z_ref[...] = jnp.dot(hb, wz_ref[...], preferred_element_type=F32).astype(BF16)
    xbc_ref[...] = jnp.dot(hb, wx_ref[...], preferred_element_type=F32).astype(BF16)
    dt_ref[...] = jnp.dot(hb, wdt_ref[...], preferred_element_type=F32)
    dtT_ref[...] = lax.dot_general(wdtT_ref[...], hb, (((1,), (1,)), ((), ())), preferred_element_type=F32)


def _inproj(x2d, gain, shift, scale, wf, wz, wx, wdt, wdtT, csc):
    tiles_per_batch = SEQ // PROJ_TM
    row = lambda i: (i, 0)
    const = lambda i: (0, 0)
    per_batch = lambda i: (i // tiles_per_batch, 0, 0)
    return pl.pallas_call(
        _inproj_kernel,
        grid=(TOKENS // PROJ_TM,),
        in_specs=[
            pl.BlockSpec((PROJ_TM, D_MODEL), row),
            pl.BlockSpec((1, D_MODEL), const),
            pl.BlockSpec((1, 1, D_MODEL), per_batch),
            pl.BlockSpec((1, 1, D_MODEL), per_batch),
            pl.BlockSpec(wf.shape, const),
            pl.BlockSpec(wz.shape, const),
            pl.BlockSpec(wx.shape, const),
            pl.BlockSpec(wdt.shape, const),
            pl.BlockSpec(wdtT.shape, const),
            pl.BlockSpec(csc.shape, const),
        ],
        out_specs=[
            pl.BlockSpec((PROJ_TM, D_FNET), row),
            pl.BlockSpec((PROJ_TM, D_FNET), row),
            pl.BlockSpec((PROJ_TM, D_SSM), row),
            pl.BlockSpec((PROJ_TM, D_CONV), row),
            pl.BlockSpec((PROJ_TM, DT_PAD), row),
            pl.BlockSpec((DT_PAD, PROJ_TM), lambda i: (0, i)),
        ],
        out_shape=[
            jax.ShapeDtypeStruct((TOKENS, D_FNET), BF16),
            jax.ShapeDtypeStruct((TOKENS, D_FNET), BF16),
            jax.ShapeDtypeStruct((TOKENS, D_SSM), BF16),
            jax.ShapeDtypeStruct((TOKENS, D_CONV), BF16),
            jax.ShapeDtypeStruct((TOKENS, DT_PAD), F32),
            jax.ShapeDtypeStruct((DT_PAD, TOKENS), F32),
        ],
        compiler_params=_cparams(("arbitrary",)),
        name="inproj",
    )(x2d, gain, shift, scale, wf, wz, wx, wdt, wdtT, csc)


def _dft_kernel(cs_ref, ss_ref, uc_ref, us_ref, o_ref):
    y = jnp.dot(cs_ref[...], uc_ref[...], preferred_element_type=F32)
    y = y - jnp.dot(ss_ref[...], us_ref[...], preferred_element_type=F32)
    o_ref[...] = y.astype(BF16)


def _dft(cs, ss, uc, us):
    mt = SEQ // DFT_TM
    return pl.pallas_call(
        _dft_kernel,
        grid=(BATCH, mt),
        in_specs=[
            pl.BlockSpec((DFT_TM, SEQ), lambda b, m: (m, 0)),
            pl.BlockSpec((DFT_TM, SEQ), lambda b, m: (m, 0)),
            pl.BlockSpec((SEQ, D_FNET), lambda b, m: (b, 0)),
            pl.BlockSpec((SEQ, D_FNET), lambda b, m: (b, 0)),
        ],
        out_specs=pl.BlockSpec((DFT_TM, D_FNET), lambda b, m: (b * mt + m, 0)),
        out_shape=jax.ShapeDtypeStruct((TOKENS, D_FNET), BF16),
        compiler_params=_cparams(("arbitrary", "arbitrary")),
        name="fnet_dft",
    )(cs, ss, uc, us)


def _conv_kernel(x_ref, w_ref, b_ref, o_ref, pad_ref):
    zeros = jnp.zeros((CONV_HALO, CONV_TC), F32)
    pad_ref[0:CONV_HALO, :] = zeros
    pad_ref[SEQ + CONV_HALO:SEQ + 2 * CONV_HALO, :] = zeros
    pad_ref[CONV_HALO:SEQ + CONV_HALO, :] = x_ref[0].astype(F32)
    w = w_ref[...]
    bias = b_ref[...]
    first = CONV_HALO - CONV_WIDTH // 2
    for i in range(SEQ // CONV_TR):
        acc = jnp.broadcast_to(bias, (CONV_TR, CONV_TC))
        for k in range(CONV_WIDTH):
            lo = first + k + i * CONV_TR
            acc = acc + pad_ref[lo:lo + CONV_TR, :] * w[k:k + 1, :]
        o_ref[0, i * CONV_TR:(i + 1) * CONV_TR, :] = (acc * _sigmoid(acc)).astype(BF16)


def _conv(xbc3, conv_w, conv_b):
    return pl.pallas_call(
        _conv_kernel,
        grid=(BATCH, D_CONV // CONV_TC),
        in_specs=[
            pl.BlockSpec((1, SEQ, CONV_TC), lambda b, j: (b, 0, j)),
            pl.BlockSpec((CONV_WIDTH, CONV_TC), lambda b, j: (0, j)),
            pl.BlockSpec((1, CONV_TC), lambda b, j: (0, j)),
        ],
        out_specs=pl.BlockSpec((1, SEQ, CONV_TC), lambda b, j: (b, 0, j)),
        out_shape=jax.ShapeDtypeStruct((BATCH, SEQ, D_CONV), BF16),
        scratch_shapes=[pltpu.VMEM((SEQ + 2 * CONV_HALO, CONV_TC), F32)],
        compiler_params=_cparams(("arbitrary", "arbitrary")),
        name="conv_silu",
    )(xbc3, conv_w, conv_b)


def _ssd_direction(xc, dtvT, cs, csT, a, aT, decay_row, state_ref, yacc_ref, head_base, forward):
    li = lax.broadcasted_iota(jnp.int32, (SSD_CHUNK, SSD_CHUNK), 0)
    si = lax.broadcasted_iota(jnp.int32, (SSD_CHUNK, SSD_CHUNK), 1)
    low_half = si < SSM_HEAD_DIM
    if forward:
        u, uT = cs, csT
        mask = li >= si
    else:
        u, uT = cs - a, csT - aT
        mask = si >= li
    pair = 2 * SSM_HEAD_DIM
    for g in range(N_BC_GROUPS):
        bc = xc[:, D_SSM + g * D_STATE:D_SSM + (g + 1) * D_STATE]
        cc = xc[:, D_SSM + D_BC + g * D_STATE:D_SSM + D_BC + (g + 1) * D_STATE]
        gmat = lax.dot_general(cc, bc, (((1,), (1,)), ((), ())), preferred_element_type=F32)
        bcT = bc.astype(F32).T
        ccf = cc.astype(F32)
        for pr in range(HEADS_PER_GROUP // 2):
            h0 = g * HEADS_PER_GROUP + 2 * pr
            xcols = slice(h0 * SSM_HEAD_DIM, h0 * SSM_HEAD_DIM + pair)
            scols = slice(pr * pair, (pr + 1) * pair)
            xpair = xc[:, xcols]
            spair = state_ref[g, :, scols]
            rhs = jnp.concatenate([xpair, spair.astype(BF16)], axis=0)
            ys, upds = [], []
            for j in range(2):
                hh = head_base + h0 + j
                ub = jnp.broadcast_to(u[:, hh:hh + 1], (SSD_CHUNK, SSD_CHUNK))
                urow = uT[hh:hh + 1, :]
                dtrow = dtvT[hh:hh + 1, :]
                tot = cs[SSD_CHUNK - 1:SSD_CHUNK, hh:hh + 1]
                if forward:
                    diff = ub - urow
                    cscale = jnp.exp(ub)
                    wrow = jnp.exp(tot - urow) * dtrow
                else:
                    diff = urow - ub
                    cscale = jnp.exp(tot - ub)
                    wrow = jnp.exp(urow) * dtrow
                lmat = jnp.where(mask, jnp.exp(diff), 0.0)
                lhs = jnp.concatenate([(gmat * lmat * dtrow).astype(BF16), (ccf * cscale).astype(BF16)], axis=1)
                ys.append(jnp.dot(lhs, rhs, preferred_element_type=F32))
                upds.append(jnp.dot((bcT * wrow).astype(BF16), xpair, preferred_element_type=F32))
            yacc_ref[:, xcols] = jnp.where(low_half, ys[0], ys[1])
            state_ref[g, :, scols] = spair * decay_row[:, xcols] + jnp.where(low_half, upds[0], upds[1])


def _ssd_kernel(xc_ref, dt_ref, dtT_ref, z_ref, alog_row_ref, bias_row_ref, alog_col_ref, bias_col_ref,
                dskip_ref, ng_ref, expand_ref, y_ref, state_ref, yacc_ref, yb_ref):
    phase = pl.program_id(1)
    step = pl.program_id(2)

    @pl.when(step == 0)
    def _():
        state_ref[...] = jnp.zeros(state_ref.shape, F32)

    li = lax.broadcasted_iota(jnp.int32, (SSD_CHUNK, SSD_CHUNK), 0)
    si = lax.broadcasted_iota(jnp.int32, (SSD_CHUNK, SSD_CHUNK), 1)
    lower = (li >= si).astype(BF16)
    upper = (li <= si).astype(BF16)
    used = 2 * N_SSM_HEADS

    def scan_chunk(sub, forward):
        rows = slice(sub * SSD_CHUNK, (sub + 1) * SSD_CHUNK)
        xc = xc_ref[rows, :]
        dtv = _softplus(dt_ref[rows, :] + bias_row_ref[...])
        a = dtv * (-jnp.exp(alog_row_ref[...]))
        dtvT = _softplus(dtT_ref[0:used, rows] + bias_col_ref[0:used, :])
        aT = dtvT * (-jnp.exp(alog_col_ref[0:used, :]))
        cs = sum(jnp.dot(lower, t, preferred_element_type=F32) for t in _bf16_terms(a))
        csT = sum(jnp.dot(t, upper, preferred_element_type=F32) for t in _bf16_terms(aT))
        chunk_decay = jnp.exp(cs[SSD_CHUNK - 1:SSD_CHUNK, :])
        expand = expand_ref[0 if forward else 1]
        decay_row = sum(jnp.dot(t, expand, preferred_element_type=F32) for t in _bf16_terms(chunk_decay))
        _ssd_direction(xc, dtvT, cs, csT, a, aT, decay_row, state_ref, yacc_ref,
                       0 if forward else N_SSM_HEADS, forward)
        return rows, xc

    @pl.when(phase == 0)
    def _():
        for sub in reversed(range(SSD_STEP_CHUNKS)):
            scan_chunk(sub, False)
            chunk = (N_CHUNKS // SSD_STEP_CHUNKS - 1 - step) * SSD_STEP_CHUNKS + sub
            yb_ref[chunk] = yacc_ref[...].astype(BF16)

    @pl.when(phase == 1)
    def _():
        for sub in range(SSD_STEP_CHUNKS):
            rows, xc = scan_chunk(sub, True)
            xs = xc[:, :D_SSM].astype(F32)
            y = yacc_ref[...] + yb_ref[step * SSD_STEP_CHUNKS + sub].astype(F32) + xs * dskip_ref[...]
            z = z_ref[rows, :].astype(F32)
            y = y * (z * _sigmoid(z))
            for g in range(N_BC_GROUPS):
                gs = slice(g * GROUP_WIDTH, (g + 1) * GROUP_WIDTH)
                yg = y[:, gs]
                ms = jnp.mean(yg * yg, axis=-1, keepdims=True)
                y_ref[rows, gs] = (yg * lax.rsqrt(ms + EPS) * ng_ref[:, gs]).astype(BF16)


def _head_expansion():
    col = lax.broadcasted_iota(jnp.int32, (2, DT_PAD, D_SSM), 1)
    head = lax.broadcasted_iota(jnp.int32, (2, DT_PAD, D_SSM), 2) // SSM_HEAD_DIM
    direction = lax.broadcasted_iota(jnp.int32, (2, DT_PAD, D_SSM), 0)
    return (col == direction * N_SSM_HEADS + head).astype(BF16)


def _ssd(xconv, dt, dtT, z, alog_row, bias_row, alog_col, bias_col, dskip_row, norm_g):
    steps = N_CHUNKS // SSD_STEP_CHUNKS
    step_rows = SSD_STEP_CHUNKS * SSD_CHUNK

    def block_of(p, s):
        return p * s + (1 - p) * (steps - 1 - s)

    rows = lambda b, p, s: (b * steps + block_of(p, s), 0)
    cols = lambda b, p, s: (0, b * steps + block_of(p, s))
    fwd_only = lambda b, p, s: (b * steps + p * s, 0)
    const = lambda b, p, s: (0, 0)
    return pl.pallas_call(
        _ssd_kernel,
        grid=(BATCH, 2, steps),
        in_specs=[
            pl.BlockSpec((step_rows, D_CONV), rows),
            pl.BlockSpec((step_rows, DT_PAD), rows),
            pl.BlockSpec((DT_PAD, step_rows), cols),
            pl.BlockSpec((step_rows, D_SSM), fwd_only),
            pl.BlockSpec((1, DT_PAD), const),
            pl.BlockSpec((1, DT_PAD), const),
            pl.BlockSpec((DT_PAD, 1), const),
            pl.BlockSpec((DT_PAD, 1), const),
            pl.BlockSpec((1, D_SSM), const),
            pl.BlockSpec((1, D_SSM), const),
            pl.BlockSpec((2, DT_PAD, D_SSM), lambda b, p, s: (0, 0, 0)),
        ],
        out_specs=pl.BlockSpec((step_rows, D_SSM), fwd_only),
        out_shape=jax.ShapeDtypeStruct((TOKENS, D_SSM), BF16),
        scratch_shapes=[
            pltpu.VMEM((N_BC_GROUPS, D_STATE, GROUP_WIDTH), F32),
            pltpu.VMEM((SSD_CHUNK, D_SSM), F32),
            pltpu.VMEM((N_CHUNKS, SSD_CHUNK, D_SSM), BF16),
        ],
        compiler_params=_cparams(("arbitrary", "arbitrary", "arbitrary")),
        name="ssd_scan",
    )(xconv, dt, dtT, z, alog_row, bias_row, alog_col, bias_col, dskip_row, norm_g, _head_expansion())


def _outproj_kernel(x_ref, yf_ref, ys_ref, wf_ref, ws_ref, gate_ref, g_ref, shift_ref, scale_ref,
                    x1_ref, h2_ref, h2T_ref):
    mix = jnp.dot(yf_ref[...], wf_ref[...], preferred_element_type=F32)
    mix = mix + jnp.dot(ys_ref[...], ws_ref[...], preferred_element_type=F32)
    x1 = x_ref[...] + gate_ref[0] * mix
    x1_ref[...] = x1
    h2 = _modulated_norm(x1, g_ref[...], shift_ref[0], scale_ref[0])
    h2_ref[...] = h2.astype(BF16)
    h2T_ref[...] = h2.T.astype(BF16)


def _outproj(x2d, yf, ys, wf, ws, gate, gain, shift, scale):
    tiles_per_batch = SEQ // OUT_TM
    row = lambda i: (i, 0)
    const = lambda i: (0, 0)
    per_batch = lambda i: (i // tiles_per_batch, 0, 0)
    return pl.pallas_call(
        _outproj_kernel,
        grid=(TOKENS // OUT_TM,),
        in_specs=[
            pl.BlockSpec((OUT_TM, D_MODEL), row),
            pl.BlockSpec((OUT_TM, D_FNET), row),
            pl.BlockSpec((OUT_TM, D_SSM), row),
            pl.BlockSpec(wf.shape, const),
            pl.BlockSpec(ws.shape, const),
            pl.BlockSpec((1, 1, D_MODEL), per_batch),
            pl.BlockSpec((1, D_MODEL), const),
            pl.BlockSpec((1, 1, D_MODEL), per_batch),
            pl.BlockSpec((1, 1, D_MODEL), per_batch),
        ],
        out_specs=[pl.BlockSpec((OUT_TM, D_MODEL), row), pl.BlockSpec((OUT_TM, D_MODEL), row),
                   pl.BlockSpec((D_MODEL, OUT_TM), lambda i: (0, i))],
        out_shape=[jax.ShapeDtypeStruct((TOKENS, D_MODEL), F32), jax.ShapeDtypeStruct((TOKENS, D_MODEL), BF16),
                   jax.ShapeDtypeStruct((D_MODEL, TOKENS), BF16)],
        compiler_params=_cparams(("arbitrary",)),
        name="outproj",
    )(x2d, yf, ys, wf, ws, gate, gain, shift, scale)


MARKER_UNIT = 2.0 ** 100


def _extract_topk_marked(x, k):
    work = x
    vals = []
    for r in range(k):
        m = jnp.max(work, axis=0, keepdims=True)
        work = jnp.where(work == m, -(k + r) * MARKER_UNIT, work)
        vals.append(m)
    marked = work <= -k * MARKER_UNIT
    rank = jnp.where(marked, work * (-1.0 / MARKER_UNIT) - float(k), float(k))
    in_range = jnp.min(x, axis=0, keepdims=True) > -MARKER_UNIT
    return jnp.concatenate(vals, axis=0), rank, in_range


def _extract_topk(x, k, break_ties):
    rows = x.shape[0]
    iota = lax.broadcasted_iota(jnp.int32, x.shape, 0).astype(F32)
    rank = jnp.full(x.shape, float(k), F32)
    work = x
    vals = []
    for r in range(k):
        m = jnp.max(work, axis=0, keepdims=True)
        sel = work == m
        if break_ties:
            first = jnp.min(jnp.where(sel, iota, float(rows)), axis=0, keepdims=True)
            sel = iota == first
        rank = jnp.where(sel, float(r), rank)
        work = jnp.where(sel, -jnp.inf, work)
        vals.append(m)
    return jnp.concatenate(vals, axis=0), rank


def _candidate_width(r):
    return PEER_TOPK // (r + 1)


def _candidates(v1, v2, pad):
    sub = lax.broadcasted_iota(jnp.int32, (SUBLANES, v1.shape[1]), 0)
    blocks = [v1[0:1] + v2, v1[1:2] + v2[0:SUBLANES]]
    for r in range(2, SUBLANES):
        blocks.append(jnp.where(sub < _candidate_width(r), v1[r:r + 1] + v2[0:SUBLANES], pad))
    blocks.append(v1[SUBLANES:] + v2[0:1])
    return jnp.concatenate(blocks, axis=0)


def _row_counts(chosen):
    counts = [jnp.sum(chosen[0:PEER_TOPK], axis=0, keepdims=True)]
    for r in range(1, SUBLANES):
        lo = PEER_TOPK + (r - 1) * SUBLANES
        counts.append(jnp.sum(chosen[lo:lo + SUBLANES], axis=0, keepdims=True))
    tail = chosen[PEER_TOPK + (SUBLANES - 1) * SUBLANES:]
    counts.extend(tail[j:j + 1] for j in range(SUBLANES))
    return counts


def _plan_head(s1, s2, break_ties):
    if break_ties:
        v1, r1 = _extract_topk(s1, PEER_TOPK, True)
        v2, r2 = _extract_topk(s2, PEER_TOPK, True)
        in_range = None
    else:
        v1, r1, ok1 = _extract_topk_marked(s1, PEER_TOPK)
        v2, r2, ok2 = _extract_topk_marked(s2, PEER_TOPK)
        in_range = jnp.logical_and(ok1, ok2)
    if break_ties:
        cand = _candidates(v1, v2, -jnp.inf)
        _, crank = _extract_topk(cand, PEER_TOPK, True)
    else:
        cand = _candidates(v1, v2, -4.0 * MARKER_UNIT)
        _, crank, _ = _extract_topk_marked(cand, PEER_TOPK)
    chosen = (crank < float(PEER_TOPK)).astype(F32)
    counts = _row_counts(chosen)
    top = v1[0:1] + v2[0:1]
    zsum = jnp.sum(chosen * jnp.exp(cand - top), axis=0, keepdims=True)
    r1b = r1.astype(BF16)
    n1 = jnp.zeros(s1.shape, BF16)
    for r in range(PEER_TOPK):
        n1 = jnp.where(r1b == float(r), counts[r].astype(BF16), n1)
    e1 = jnp.exp(s1 - v1[0:1]) / zsum
    e2 = jnp.exp(s2 - v2[0:1])
    ranked = (r1 < float(PEER_TOPK)).astype(F32) + (r2 < float(PEER_TOPK)).astype(F32)
    total = jnp.sum(ranked, axis=0, keepdims=True) + jnp.sum(chosen, axis=0, keepdims=True)
    good = total == 3.0 * PEER_TOPK
    if in_range is not None:
        good = jnp.logical_and(good, in_range)
    clean = jnp.min(jnp.where(good, 1.0, 0.0)) > 0.5
    return r2, e2, n1.astype(F32), e1, clean


def _twin_bf16_words(x):
    hi = pltpu.bitcast(x.astype(BF16).astype(F32), jnp.uint32)
    return hi | (hi >> 16)


def _plan_kernel(h2_ref, wq_ref, keys_ref, c2_ref, e2_ref, n1_ref, e1_ref, q_ref):
    q_ref[...] = jnp.dot(h2_ref[...], wq_ref[...], preferred_element_type=F32).astype(BF16)

    def per_head(h, carry):
        def scores(side):
            col = pl.multiple_of((2 * h + side) * PEER_HALF, PEER_HALF)
            qh = q_ref[:, pl.ds(col, PEER_HALF)]
            return lax.dot_general(keys_ref[2 * h + side], qh, (((1,), (1,)), ((), ())), preferred_element_type=F32)

        s1 = scores(0)
        s2 = scores(1)

        def store(r2, e2, n1, e1):
            c2_ref[h] = r2.astype(BF16)
            e2_ref[h] = e2.astype(BF16)
            for c in range(PLAN_TT // LANES):
                cols = slice(c * LANES, (c + 1) * LANES)
                n1_ref[h, c] = _twin_bf16_words(n1[:, cols])
                e1_ref[h, c] = _twin_bf16_words(e1[:, cols])

        r2, e2, n1, e1, clean = _plan_head(s1, s2, break_ties=False)
        store(r2, e2, n1, e1)

        @pl.when(jnp.logical_not(clean))
        def _():
            store(*_plan_head(s1, s2, break_ties=True)[:4])

        return carry

    lax.fori_loop(0, PEER_HEADS, per_head, 0)


def _plan(h2, wq, keys):
    blk = lambda i: (0, 0, i)
    shp = (PEER_HEADS, PEER_KEYS, TOKENS)
    lane_blk = lambda i: (0, i, 0, 0)
    lane_shp = (PEER_HEADS, TOKENS // LANES, PEER_KEYS, LANES)
    lane_spec = pl.BlockSpec((PEER_HEADS, PLAN_TT // LANES, PEER_KEYS, LANES), lane_blk)
    return pl.pallas_call(
        _plan_kernel,
        grid=(TOKENS // PLAN_TT,),
        in_specs=[
            pl.BlockSpec((PLAN_TT, D_MODEL), lambda i: (i, 0)),
            pl.BlockSpec(wq.shape, lambda i: (0, 0)),
            pl.BlockSpec(keys.shape, lambda i: (0, 0, 0)),
        ],
        out_specs=[pl.BlockSpec((PEER_HEADS, PEER_KEYS, PLAN_TT), blk)] * 2 + [lane_spec] * 2,
        out_shape=[
            jax.ShapeDtypeStruct(shp, BF16),
            jax.ShapeDtypeStruct(shp, BF16),
            jax.ShapeDtypeStruct(lane_shp, jnp.uint32),
            jax.ShapeDtypeStruct(lane_shp, jnp.uint32),
        ],
        scratch_shapes=[pltpu.VMEM((PLAN_TT, 2 * PEER_HEADS * PEER_HALF), BF16)],
        compiler_params=_cparams(("arbitrary",)),
        name="peer_plan",
    )(h2, wq, keys)


def _gelu_bf16(x):
    half_x = x.astype(BF16) * 0.5
    return half_x + half_x * lax.erf(x * (1.0 / math.sqrt(2.0))).astype(BF16)


DENSE_NE = PEER_EXPERTS // DENSE_TE
DENSE_NT = TOKENS // DENSE_TT
DENSE_TILES = DENSE_NT * DENSE_NE
DENSE_KEYS_PER_TILE = DENSE_TE // PEER_KEYS


def _dense_stage_body(f, h2_ref, down_ref, upT_ref, c2_ref, e2_ref, n1_ref, e1_ref, acc_ref, act_w, act_r):
    gate_tile = jnp.clip(f - 1, 0, DENSE_TILES - 1) % DENSE_NE

    def act_piece(q, c):
        rows = slice(q * DENSE_SUB, (q + 1) * DENSE_SUB)
        cols = slice(c * DENSE_CHUNK, (c + 1) * DENSE_CHUNK)
        act_w[rows, cols] = jnp.dot(down_ref[rows, :], h2_ref[:, cols], preferred_element_type=F32)

    def gate_piece(k, c):
        i1 = gate_tile * DENSE_KEYS_PER_TILE + k
        cols = slice(c * DENSE_CHUNK, (c + 1) * DENSE_CHUNK)
        rows = slice(k * PEER_KEYS, (k + 1) * PEER_KEYS)

        def row_tile(ref, h):
            tiles = []
            for t in range(DENSE_CHUNK // LANES):
                lane_tile = c * (DENSE_CHUNK // LANES) + t
                words = jnp.broadcast_to(ref[h, lane_tile, pl.ds(i1, 1), :], (SUBLANES, LANES))
                tiles.append(jnp.concatenate([pltpu.bitcast(words, BF16)] * (PEER_KEYS // BF16_ROWS), axis=0))
            return jnp.concatenate(tiles, axis=1)

        gates = jnp.zeros((PEER_KEYS, DENSE_CHUNK), BF16)
        for h in range(PEER_HEADS):
            hit = c2_ref[h, :, cols] < row_tile(n1_ref, h)
            gates = gates + jnp.where(hit, e2_ref[h, :, cols], jnp.zeros((), BF16)) * row_tile(e1_ref, h)
        return gates * _gelu_bf16(act_r[rows, cols])

    keys_per_sub = DENSE_SUB // PEER_KEYS
    n_sub = DENSE_TE // DENSE_SUB
    for c in range(DENSE_TT // DENSE_CHUNK):
        cols = slice(c * DENSE_CHUNK, (c + 1) * DENSE_CHUNK)
        part = None
        for j in range(n_sub):
            w = jnp.concatenate([gate_piece(keys_per_sub * j + k, c) for k in range(keys_per_sub)], axis=0)
            d = jnp.dot(upT_ref[:, j * DENSE_SUB:(j + 1) * DENSE_SUB], w, preferred_element_type=F32)
            part = d if part is None else part + d
            act_piece(j, c)
        acc_ref[:, cols] += part


def _dense_kernel(h2_ref, down_ref, upT_ref, c2_ref, e2_ref, n1_ref, e1_ref, x1_ref, gate_ref, g_ref,
                  o_ref, acc_ref, act0_ref, act1_ref):
    f = pl.program_id(0)
    out_tile = jnp.clip(f - 1, 0, DENSE_TILES - 1) % DENSE_NE

    @pl.when(f == 0)
    def _():
        act1_ref[...] = jnp.zeros(act1_ref.shape, F32)

    @pl.when(out_tile == 0)
    def _():
        acc_ref[...] = jnp.zeros(acc_ref.shape, F32)

    stage = functools.partial(_dense_stage_body, f, h2_ref, down_ref, upT_ref, c2_ref, e2_ref, n1_ref, e1_ref,
                              acc_ref)

    @pl.when(f % 2 == 0)
    def _():
        stage(act0_ref, act1_ref)

    @pl.when(f % 2 == 1)
    def _():
        stage(act1_ref, act0_ref)

    @pl.when(jnp.logical_and(out_tile == DENSE_NE - 1, f >= 1))
    def _():
        x2 = x1_ref[...] + gate_ref[0] * acc_ref[...].T
        ms = jnp.mean(x2 * x2, axis=-1, keepdims=True)
        o_ref[...] = x2 * lax.rsqrt(ms + EPS) * g_ref[...]


def _dense(h2T, down, upT, c2, e2, n1, e1, x1, gate, final_g):
    tiles_per_batch = SEQ // DENSE_TT

    def tile(f, lag):
        return jnp.clip(f - lag, 0, DENSE_TILES - 1)

    plan = lambda f: (0, tile(f, 1) // DENSE_NE, 0, 0)
    out_tok = lambda f: (tile(f, 1) // DENSE_NE, 0)
    plan_blk = (PEER_HEADS, DENSE_TT // LANES, PEER_KEYS, LANES)
    rank_map = lambda f: (0, 0, tile(f, 1) // DENSE_NE)
    rank_blk = (PEER_HEADS, PEER_KEYS, DENSE_TT)
    return pl.pallas_call(
        _dense_kernel,
        grid=(DENSE_TILES + 1,),
        in_specs=[
            pl.BlockSpec((D_MODEL, DENSE_TT), lambda f: (0, tile(f, 0) // DENSE_NE)),
            pl.BlockSpec((DENSE_TE, D_MODEL), lambda f: (tile(f, 0) % DENSE_NE, 0)),
            pl.BlockSpec((D_MODEL, DENSE_TE), lambda f: (0, tile(f, 1) % DENSE_NE)),
            pl.BlockSpec(rank_blk, rank_map),
            pl.BlockSpec(rank_blk, rank_map),
            pl.BlockSpec(plan_blk, plan),
            pl.BlockSpec(plan_blk, plan),
            pl.BlockSpec((DENSE_TT, D_MODEL), out_tok),
            pl.BlockSpec((1, 1, D_MODEL), lambda f: (tile(f, 1) // DENSE_NE // tiles_per_batch, 0, 0)),
            pl.BlockSpec((1, D_MODEL), lambda f: (0, 0)),
        ],
        out_specs=pl.BlockSpec((DENSE_TT, D_MODEL), out_tok),
        out_shape=jax.ShapeDtypeStruct((TOKENS, D_MODEL), F32),
        scratch_shapes=[
            pltpu.VMEM((D_MODEL, DENSE_TT), F32),
            pltpu.VMEM((DENSE_TE, DENSE_TT), F32),
            pltpu.VMEM((DENSE_TE, DENSE_TT), F32),
        ],
        compiler_params=_cparams(("arbitrary",)),
        name="peer_dense",
    )(h2T, down, upT, c2, e2, n1, e1, x1, gate, final_g)


def _transpose_kernel(x_ref, o_ref):
    o_ref[...] = x_ref[...].T.astype(BF16)


def _transpose_table(t):
    rows, cols = t.shape
    tr = 512
    return pl.pallas_call(
        _transpose_kernel,
        grid=(rows // tr,),
        in_specs=[pl.BlockSpec((tr, cols), lambda i: (i, 0))],
        out_specs=pl.BlockSpec((cols, tr), lambda i: (0, i)),
        out_shape=jax.ShapeDtypeStruct((cols, rows), BF16),
        compiler_params=_cparams(("arbitrary",)),
        name="table_transpose",
    )(t)


def _position_dft_tables():
    radix = 64
    r = lax.broadcasted_iota(jnp.int32, (radix, SEQ), 0)
    k = lax.broadcasted_iota(jnp.int32, (radix, SEQ), 1)
    coarse = ((r * k) % radix).astype(F32) * (2.0 * math.pi / radix)
    fine = ((r * k) % SEQ).astype(F32) * (2.0 * math.pi / SEQ)
    scale = 1.0 / math.sqrt(SEQ)
    ca, sa = jnp.cos(coarse)[:, None, :] * scale, jnp.sin(coarse)[:, None, :] * scale
    cb, sb = jnp.cos(fine)[None, :, :], jnp.sin(fine)[None, :, :]
    cos_tab = (ca * cb - sa * sb).reshape(SEQ, SEQ).astype(BF16)
    sin_tab = (sa * cb + ca * sb).reshape(SEQ, SEQ).astype(BF16)
    return cos_tab, sin_tab


def _channel_dft_table():
    j = lax.broadcasted_iota(jnp.int32, (FNET_GROUP, FNET_GROUP), 0)
    k = lax.broadcasted_iota(jnp.int32, (FNET_GROUP, FNET_GROUP), 1)
    ang = ((j * k) % FNET_GROUP).astype(F32) * (2.0 * math.pi / FNET_GROUP)
    scale = 1.0 / math.sqrt(FNET_GROUP)
    return jnp.concatenate([jnp.cos(ang) * scale, jnp.sin(ang) * scale], axis=1).astype(BF16)


def _pad_lanes(v, width):
    return jnp.pad(v, ((0, 0), (0, width - v.shape[1])))


def kernel(x, c, w_ada, b_ada, norm_mix_g, w_in, conv_w, conv_b, a_log_fwd, a_log_bwd, dt_bias_fwd, dt_bias_bwd, d_skip, ssm_norm_g, w_out, norm_ffn_g, w_query, sub_keys, expert_down, expert_up, final_norm_g):
    assert w_ada.shape[0] == 1, "single-layer problem: the final RMSNorm is fused into the PEER kernel"
    xt = x.reshape(TOKENS, D_MODEL)
    c_pad = jnp.pad(c, ((0, SUBLANES - BATCH), (0, 0)))
    cs_tab, ss_tab = _position_dft_tables()
    csc_tab = _channel_dft_table()
    for layer in range(1):
        mod = _adaln(c_pad, w_ada[layer], b_ada[layer][None, :])[:BATCH]
        shift_m, scale_m, gate_m, shift_f, scale_f, gate_f = [
            m.reshape(BATCH, 1, D_MODEL) for m in jnp.split(mod, 6, axis=-1)]

        w = w_in[layer]
        wf = w[:, :D_FNET].astype(BF16)
        wz = w[:, D_FNET:D_FNET + D_SSM].astype(BF16)
        wx = w[:, D_FNET + D_SSM:D_FNET + D_SSM + D_CONV].astype(BF16)
        wdt = _pad_lanes(w[:, D_FNET + D_SSM + D_CONV:], DT_PAD).astype(BF16)
        uc, us, z, xbc, dt, dtT = _inproj(xt, norm_mix_g[layer][None, :], shift_m, scale_m,
                                          wf, wz, wx, wdt, wdt.T, csc_tab)
        y_fnet = _dft(cs_tab, ss_tab, uc, us)

        xconv = _conv(xbc.reshape(BATCH, SEQ, D_CONV), conv_w[layer], conv_b[layer][None, :])
        alog_row = _pad_lanes(jnp.concatenate([a_log_fwd[layer], a_log_bwd[layer]])[None, :], DT_PAD)
        bias_row = _pad_lanes(jnp.concatenate([dt_bias_fwd[layer], dt_bias_bwd[layer]])[None, :], DT_PAD)
        dskip_row = jnp.repeat(d_skip[layer], SSM_HEAD_DIM)[None, :]
        y_ssm = _ssd(xconv.reshape(TOKENS, D_CONV), dt, dtT, z, alog_row, bias_row, alog_row.T, bias_row.T,
                     dskip_row, ssm_norm_g[layer][None, :])

        wo = w_out[layer].astype(BF16)
        x1, h2, h2T = _outproj(xt, y_fnet, y_ssm, wo[:D_FNET], wo[D_FNET:], gate_m,
                          norm_ffn_g[layer][None, :], shift_f, scale_f)

        keys = sub_keys[layer].reshape(2 * PEER_HEADS, PEER_KEYS, PEER_HALF).astype(BF16)
        c2, e2, n1, e1 = _plan(h2, w_query[layer].astype(BF16), keys)
        xt = _dense(h2T, expert_down[layer].astype(BF16), _transpose_table(expert_up[layer]),
                    c2, e2, n1, e1, x1, gate_f, final_norm_g[None, :])
    return xt.reshape(BATCH, SEQ, D_MODEL)
```
